```python
import math, functools
import jax, jax.numpy as jnp
from jax import lax
import numpy as np

D_MODEL = 1024
BATCH = 32
SEQ = 256
DEPTH = 4
DEC_BATCH = 2
DEC_SEQ = 1024
PAST_LEN = 256

GRID_W = 64
N_ATTN_LAYERS = (DEPTH + 1) // 2
N_DELTA_LAYERS = DEPTH // 2
HEAD_DIM = 128
N_HEADS = D_MODEL // HEAD_DIM
KV_HEADS = 2
GROUP = N_HEADS // KV_HEADS
Q_WIDTH = N_HEADS * HEAD_DIM
KV_WIDTH = KV_HEADS * HEAD_DIM
Q_BLOCK = 128
ROPE_THETA = 10000.0
ROPE_FREQS = HEAD_DIM // 4
DN_DK = 128
DN_DV = 128
DN_HEADS = D_MODEL // DN_DV
DN_KEY = DN_HEADS * DN_DK
DN_VAL = DN_HEADS * DN_DV
DN_QKV = 2 * DN_KEY + DN_VAL
DN_IN = DN_QKV + DN_VAL + 4 * DN_HEADS
CONV_K = 5
CHUNK = 64
D_FF = 4 * D_MODEL
N_MOD = 6
EPS = 1e-6

kernel_name = 'hybrid_diffusion_gqa_deltanet_step'


def rms_norm(x, gain):
    xf = x.astype(jnp.float32)
    y = xf * lax.rsqrt(jnp.mean(xf * xf, axis=-1, keepdims=True) + EPS)
    return (y * gain.astype(jnp.float32)).astype(x.dtype)


def l2_norm(x):
    xf = x.astype(jnp.float32)
    return xf * lax.rsqrt(jnp.sum(xf * xf, axis=-1, keepdims=True) + EPS)


def modulation(cond, w_mod, b_mod):
    m = jax.nn.silu(cond) @ w_mod + b_mod
    return jnp.split(m[..., None, :], N_MOD, axis=-1)


def trunk_layer(x, mods, gain1, gain2, w1, w2, mixer):
    shift1, scale1, gate1, shift2, scale2, gate2 = mods
    mix, aux = mixer(rms_norm(x, gain1) * (1 + scale1) + shift1)
    x = x + gate1 * mix
    h = rms_norm(x, gain2) * (1 + scale2) + shift2
    x = x + gate2 * (jnp.square(jax.nn.relu(h @ w1)) @ w2)
    return x, aux


def axial_angles(n_tokens):
    rows = n_tokens // GRID_W
    row = jnp.repeat(jnp.arange(rows), GRID_W).astype(jnp.float32)
    col = jnp.tile(jnp.arange(GRID_W), rows).astype(jnp.float32)
    inv = ROPE_THETA ** (-jnp.arange(ROPE_FREQS, dtype=jnp.float32) / ROPE_FREQS)
    return jnp.stack([row[:, None] * inv, col[:, None] * inv], axis=1)


def axial_rope(x, ang):
    xs = x.reshape(*x.shape[:-1], 2, 2, ROPE_FREQS)
    x1, x2 = xs[..., 0, :], xs[..., 1, :]
    cos = jnp.cos(ang)[:, None].astype(x.dtype)
    sin = jnp.sin(ang)[:, None].astype(x.dtype)
    out = jnp.stack([x1 * cos - x2 * sin, x2 * cos + x1 * sin], axis=-2)
    return out.reshape(x.shape)


def attn_qkv(h, w_in, q_gain, k_gain):
    B, T, _ = h.shape
    proj = h @ w_in
    q = proj[..., :Q_WIDTH].reshape(B, T, N_HEADS, HEAD_DIM)
    k = proj[..., Q_WIDTH:Q_WIDTH + KV_WIDTH].reshape(B, T, KV_HEADS, HEAD_DIM)
    v = proj[..., Q_WIDTH + KV_WIDTH:].reshape(B, T, KV_HEADS, HEAD_DIM)
    return rms_norm(q, q_gain), rms_norm(k, k_gain), v


def block_attention(q, k, v):
    B, Tq, _, _ = q.shape
    nb = Tq // Q_BLOCK
    qb = q.reshape(B, nb, Q_BLOCK, KV_HEADS, GROUP, HEAD_DIM).transpose(1, 0, 2, 3, 4, 5)
    scale = HEAD_DIM ** -0.5

    def one_block(qi):
        s = jnp.einsum('bqkgd,bskd->bkgqs', qi, k).astype(jnp.float32) * scale
        p = jax.nn.softmax(s, axis=-1).astype(v.dtype)
        return jnp.einsum('bkgqs,bskd->bqkgd', p, v)

    o = lax.map(one_block, qb)
    return o.transpose(1, 0, 2, 3, 4, 5).reshape(B, Tq, Q_WIDTH)


def context_attention(h, w_in, q_gain, k_gain, w_out):
    q, k, v = attn_qkv(h, w_in, q_gain, k_gain)
    return block_attention(q, k, v) @ w_out, (k, v)


def latent_attention(h, ck, cv, ang, w_in, q_gain, k_gain, w_out):
    q, k, v = attn_qkv(h, w_in, q_gain, k_gain)
    q, k = axial_rope(q, ang), axial_rope(k, ang)
    keys = jnp.concatenate([ck.astype(k.dtype), k], axis=1)
    vals = jnp.concatenate([cv.astype(v.dtype), v], axis=1)
    return block_attention(q, keys, vals) @ w_out, None


def centred_depthwise_conv(x, w):
    return lax.conv_general_dilated(
        x, w[:, None, :].astype(x.dtype), window_strides=(1,),
        padding=[(CONV_K // 2, CONV_K // 2)], dimension_numbers=('NWC', 'WIO', 'NWC'),
        feature_group_count=x.shape[-1])


def chunk_gated_delta(q, k, v, g, beta, s0):
    B, T, H, DK = q.shape
    DV = v.shape[-1]
    N = T // CHUNK
    f32 = jnp.float32
    to_chunks = lambda a: a.astype(f32).reshape(B, N, CHUNK, H, a.shape[-1]).transpose(1, 0, 3, 2, 4)
    q, k, v = to_chunks(q) * (DK ** -0.5), to_chunks(k), to_chunks(v)
    g = g.astype(f32).reshape(B, N, CHUNK, H).transpose(1, 0, 3, 2)
    beta = beta.astype(f32).reshape(B, N, CHUNK, H).transpose(1, 0, 3, 2)
    gc = jnp.cumsum(g, axis=-1)
    tril = jnp.tril(jnp.ones((CHUNK, CHUNK), bool))
    strict = jnp.tril(jnp.ones((CHUNK, CHUNK), bool), -1)
    diff = gc[..., :, None] - gc[..., None, :]
    decay = jnp.where(tril, jnp.exp(jnp.where(tril, diff, 0.0)), 0.0)
    kb = k * beta[..., None]
    vb = v * beta[..., None]
    lower = jnp.where(strict, jnp.einsum('nbhcd,nbhed->nbhce', kb, k) * decay, 0.0)
    eye = jnp.eye(CHUNK, dtype=f32)
    tinv = lax.linalg.triangular_solve(eye + lower, jnp.broadcast_to(eye, lower.shape),
                                       left_side=True, lower=True, unit_diagonal=True)
    u = tinv @ vb
    w = tinv @ (kb * jnp.exp(gc)[..., None])
    qk = jnp.where(tril, jnp.einsum('nbhcd,nbhed->nbhce', q, k) * decay, 0.0)

    def step(S, xs):
        q_i, k_i, u_i, w_i, qk_i, gc_i = xs
        v_new = u_i - w_i @ S
        o = (q_i * jnp.exp(gc_i)[..., None]) @ S + qk_i @ v_new
        g_last = gc_i[..., -1]
        S = S * jnp.exp(g_last)[..., None, None] + jnp.einsum(
            'bhcd,bhce->bhde', k_i * jnp.exp(g_last[..., None] - gc_i)[..., None], v_new)
        return S, o

    S, o = lax.scan(step, s0.astype(f32), (q, k, u, w, qk, gc))
    return o.transpose(1, 0, 3, 2, 4).reshape(B, T, H, DV), S


def delta_mixer(h, s0, w_in, conv_w, a_log, dt_bias, norm_gain, w_out):
    B, T, _ = h.shape
    proj = h @ w_in
    qkv, z, a, b = jnp.split(proj, [DN_QKV, DN_QKV + DN_VAL, DN_QKV + DN_VAL + 2 * DN_HEADS], axis=-1)
    qkv = jax.nn.silu(centred_depthwise_conv(qkv, conv_w))
    q, k, v = jnp.split(qkv, [DN_KEY, 2 * DN_KEY], axis=-1)
    q = l2_norm(q.reshape(B, T, DN_HEADS, DN_DK))
    k = l2_norm(k.reshape(B, T, DN_HEADS, DN_DK))
    v = v.reshape(B, T, DN_HEADS, DN_DV)
    a = a.reshape(B, T, 2, DN_HEADS).astype(jnp.float32)
    b = b.reshape(B, T, 2, DN_HEADS).astype(jnp.float32)
    g = -jnp.exp(a_log.astype(jnp.float32)) * jax.nn.softplus(a + dt_bias.astype(jnp.float32))
    beta = jax.nn.sigmoid(b)
    o_f, s_f = chunk_gated_delta(q, k, v, g[:, :, 0], beta[:, :, 0], s0[:, 0])
    rev = lambda t: jnp.flip(t, axis=1)
    o_b, s_b = chunk_gated_delta(rev(q), rev(k), rev(v), rev(g[:, :, 1]), rev(beta[:, :, 1]), s0[:, 1])
    o = o_f + rev(o_b)
    o = rms_norm(o, norm_gain) * jax.nn.silu(z.reshape(B, T, DN_HEADS, DN_DV).astype(jnp.float32))
    out = o.reshape(B, T, DN_VAL).astype(h.dtype) @ w_out
    return out, jnp.stack([s_f, s_b], axis=1)


def setup_inputs(seed: int = 0) -> dict:
    key = jax.random.key(seed)
    ks = jax.random.split(key, 24)
    f32 = jnp.float32
    D = D_MODEL
    nrm = lambda k, shape, s: jax.random.normal(k, shape, f32) * s
    dn_a_log = jnp.log(jax.random.uniform(ks[17], (N_DELTA_LAYERS, 2, DN_HEADS), f32, 1.0, 16.0))
    dt = jnp.exp(jax.random.uniform(ks[18], (N_DELTA_LAYERS, 2, DN_HEADS), f32,
                                    math.log(1e-3), math.log(1e-1)))
    dn_dt_bias = dt + jnp.log(-jnp.expm1(-dt))
    return {
        'x_prompt': nrm(ks[0], (BATCH, SEQ, D), 1.0),
        'x_sample': nrm(ks[1], (DEC_BATCH, DEC_SEQ, D), 1.0),
        'cache_k': nrm(ks[2], (DEC_BATCH, N_ATTN_LAYERS, PAST_LEN, KV_HEADS, HEAD_DIM), 1.0),
        'cache_v': nrm(ks[3], (DEC_BATCH, N_ATTN_LAYERS, PAST_LEN, KV_HEADS, HEAD_DIM), 1.0),
        'state_delta': nrm(ks[4], (DEC_BATCH, N_DELTA_LAYERS, 2, DN_HEADS, DN_DK, DN_DV), 0.1),
        'c': nrm(ks[5], (DEC_BATCH, D), 1.0),
        'c_ctx': nrm(ks[6], (D,), 1.0),
        'w_mod': nrm(ks[7], (DEPTH, D, N_MOD * D), 0.5 * D ** -0.5),
        'b_mod': nrm(ks[8], (DEPTH, N_MOD * D), 0.02),
        'norm1': 1.0 + nrm(ks[9], (DEPTH, D), 0.02),
        'norm2': 1.0 + nrm(ks[10], (DEPTH, D), 0.02),
        'attn_w_in': nrm(ks[11], (N_ATTN_LAYERS, D, Q_WIDTH + 2 * KV_WIDTH), D ** -0.5),
        'attn_q_gain': 1.0 + nrm(ks[12], (N_ATTN_LAYERS, HEAD_DIM), 0.02),
        'attn_k_gain': 1.0 + nrm(ks[13], (N_ATTN_LAYERS, HEAD_DIM), 0.02),
        'attn_w_out': nrm(ks[14], (N_ATTN_LAYERS, Q_WIDTH, D), Q_WIDTH ** -0.5),
        'dn_w_in': nrm(ks[15], (N_DELTA_LAYERS, D, DN_IN), D ** -0.5),
        'dn_conv': nrm(ks[16], (N_DELTA_LAYERS, CONV_K, DN_QKV), CONV_K ** -0.5),
        'dn_a_log': dn_a_log,
        'dn_dt_bias': dn_dt_bias,
        'dn_norm': 1.0 + nrm(ks[19], (N_DELTA_LAYERS, DN_DV), 0.02),
        'dn_w_out': nrm(ks[20], (N_DELTA_LAYERS, DN_VAL, D), DN_VAL ** -0.5),
        'mlp_w1': nrm(ks[21], (DEPTH, D, D_FF), D ** -0.5),
        'mlp_w2': nrm(ks[22], (DEPTH, D_FF, D), D_FF ** -0.5),
        'final_norm': 1.0 + nrm(ks[23], (D,), 0.02),
    }


def reference(x_prompt, x_sample, cache_k, cache_v, state_delta, c, c_ctx,
              w_mod, b_mod, norm1, norm2,
              attn_w_in, attn_q_gain, attn_k_gain, attn_w_out,
              dn_w_in, dn_conv, dn_a_log, dn_dt_bias, dn_norm, dn_w_out,
              mlp_w1, mlp_w2, final_norm):
    x = x_prompt
    bsz = x.shape[0]
    keys_out, vals_out, states_out = [], [], []
    for l in range(DEPTH):
        j = l // 2
        mods = modulation(c_ctx, w_mod[l], b_mod[l])
        if l % 2 == 0:
            mixer = functools.partial(context_attention, w_in=attn_w_in[j], q_gain=attn_q_gain[j],
                                      k_gain=attn_k_gain[j], w_out=attn_w_out[j])
        else:
            s0 = jnp.zeros((bsz, 2, DN_HEADS, DN_DK, DN_DV), jnp.float32)
            mixer = functools.partial(delta_mixer, s0=s0, w_in=dn_w_in[j], conv_w=dn_conv[j],
                                      a_log=dn_a_log[j], dt_bias=dn_dt_bias[j],
                                      norm_gain=dn_norm[j], w_out=dn_w_out[j])
        x, aux = trunk_layer(x, mods, norm1[l], norm2[l], mlp_w1[l], mlp_w2[l], mixer)
        if l % 2 == 0:
            keys_out.append(aux[0])
            vals_out.append(aux[1])
        else:
            states_out.append(aux)
    y_prompt = rms_norm(x, final_norm)
    new_cache_k = jnp.stack(keys_out, axis=1)
    new_cache_v = jnp.stack(vals_out, axis=1)
    new_state_delta = jnp.stack(states_out, axis=1).astype(x_prompt.dtype)

    ang = axial_angles(x_sample.shape[1])
    x = x_sample
    for l in range(DEPTH):
        j = l // 2
        mods = modulation(c, w_mod[l], b_mod[l])
        if l % 2 == 0:
            mixer = functools.partial(latent_attention, ck=cache_k[:, j], cv=cache_v[:, j], ang=ang,
                                      w_in=attn_w_in[j], q_gain=attn_q_gain[j],
                                      k_gain=attn_k_gain[j], w_out=attn_w_out[j])
        else:
            mixer = functools.partial(delta_mixer, s0=state_delta[:, j], w_in=dn_w_in[j],
                                      conv_w=dn_conv[j], a_log=dn_a_log[j], dt_bias=dn_dt_bias[j],
                                      norm_gain=dn_norm[j], w_out=dn_w_out[j])
        x, _ = trunk_layer(x, mods, norm1[l], norm2[l], mlp_w1[l], mlp_w2[l], mixer)
    y_sample = rms_norm(x, final_norm)
    return (y_prompt, y_sample, new_cache_k, new_cache_v, new_state_delta)
```

```python
import functools

import jax
import jax.numpy as jnp
from jax import lax
from jax.experimental import pallas as pl
from jax.experimental.pallas import tpu as pltpu

F32 = jnp.float32
BF16 = jnp.bfloat16

D_MODEL = 1024
DEPTH = 4
GRID_W = 64
HEAD_DIM = 128
N_HEADS = 8
KV_HEADS = 2
GROUP = N_HEADS // KV_HEADS
Q_WIDTH = N_HEADS * HEAD_DIM
KV_WIDTH = KV_HEADS * HEAD_DIM
ROPE_THETA = 10000.0
ROPE_FREQS = HEAD_DIM // 4
DN_HEADS = 8
DN_DK = 128
DN_KEY = DN_HEADS * DN_DK
DN_MAIN = 4 * DN_KEY
CONV_K = 5
CHUNK = 64
D_FF = 4 * D_MODEL
N_MOD = 6
EPS = 1e-6
N_COND = 8
LANES = 128
VMEM_LIMIT = 52 * 1024 * 1024


def _params(n_grid):
    return pltpu.CompilerParams(dimension_semantics=("arbitrary",) * n_grid,
                                vmem_limit_bytes=VMEM_LIMIT)


def _const_spec(shape):
    nd = len(shape)
    return pl.BlockSpec(shape, lambda *_: (0,) * nd)


def _dot(a, b):
    return jnp.dot(a, b, preferred_element_type=F32)


def _dot_nt(a, b):
    return lax.dot_general(a, b, (((1,), (1,)), ((), ())), preferred_element_type=F32)


def _silu(x):
    return x * jax.nn.sigmoid(x)


def _rms(x, gain):
    return x * lax.rsqrt(jnp.mean(x * x, axis=-1, keepdims=True) + EPS) * gain


def _mod_kernel(cond_ref, w_ref, b_ref, o_ref):
    s = _silu(cond_ref[...]).astype(BF16)
    o_ref[...] = _dot(s, w_ref[...].astype(BF16)) + b_ref[...]


def _modulation(cond, w_mod, b_mod):
    tn = 1536
    n_out = N_MOD * D_MODEL
    return pl.pallas_call(
        _mod_kernel,
        grid=(DEPTH, n_out // tn),
        in_specs=[
            _const_spec((N_COND, D_MODEL)),
            pl.BlockSpec((None, D_MODEL, tn), lambda l, n: (l, 0, n)),
            pl.BlockSpec((None, 1, tn), lambda l, n: (l, 0, n)),
        ],
        out_specs=pl.BlockSpec((None, N_COND, tn), lambda l, n: (l, 0, n)),
        out_shape=jax.ShapeDtypeStruct((DEPTH, N_COND, n_out), F32),
        compiler_params=_params(2),
        name="modulation",
    )(cond, w_mod, b_mod.reshape(DEPTH, 1, n_out))


def _mods_spec(cond_of_tile):
    return pl.BlockSpec((None, N_MOD, D_MODEL), lambda i: (cond_of_tile(i), 0, 0))


def _attn_proj_kernel(x_ref, mods_ref, gain_ref, w_ref, qg_ref, kg_ref, cos_ref, sin_ref,
                      q_out, k_out, v_out):
    x = x_ref[...]
    h = _rms(x, gain_ref[...]) * (1.0 + mods_ref[1:2, :]) + mods_ref[0:1, :]
    proj = _dot(h.astype(BF16), w_ref[...])
    cos = cos_ref[...]
    sin = sin_ref[...]
    lane = lax.broadcasted_iota(jnp.int32, cos.shape, 1)
    low_half = (lane % (2 * ROPE_FREQS)) < ROPE_FREQS

    def prep(xh, gain):
        y = _rms(xh, gain)
        partner = jnp.where(low_half, pltpu.roll(y, LANES - ROPE_FREQS, 1), pltpu.roll(y, ROPE_FREQS, 1))
        return y * cos + partner * sin

    for hh in range(N_HEADS):
        sl = slice(hh * HEAD_DIM, (hh + 1) * HEAD_DIM)
        q_out[:, sl] = (prep(proj[:, sl], qg_ref[...]) * (HEAD_DIM ** -0.5)).astype(BF16)
    for kv in range(KV_HEADS):
        sl = slice(kv * HEAD_DIM, (kv + 1) * HEAD_DIM)
        k_out[:, sl] = prep(proj[:, Q_WIDTH + kv * HEAD_DIM:Q_WIDTH + (kv + 1) * HEAD_DIM], kg_ref[...])
    v_out[...] = proj[:, Q_WIDTH + KV_WIDTH:]


def _attn_proj(x, mods_l, cond_of_tile, gain, w_bf, q_gain, k_gain, rope_cos, rope_sin, rope_of_tile, tm):
    rows = x.shape[0]
    return pl.pallas_call(
        _attn_proj_kernel,
        grid=(rows // tm,),
        in_specs=[
            pl.BlockSpec((tm, D_MODEL), lambda i: (i, 0)),
            _mods_spec(cond_of_tile),
            _const_spec((1, D_MODEL)),
            _const_spec((D_MODEL, Q_WIDTH + 2 * KV_WIDTH)),
            _const_spec((1, HEAD_DIM)),
            _const_spec((1, HEAD_DIM)),
            pl.BlockSpec((tm, HEAD_DIM), lambda i: (rope_of_tile(i), 0)),
            pl.BlockSpec((tm, HEAD_DIM), lambda i: (rope_of_tile(i), 0)),
        ],
        out_specs=[
            pl.BlockSpec((tm, Q_WIDTH), lambda i: (i, 0)),
            pl.BlockSpec((tm, KV_WIDTH), lambda i: (i, 0)),
            pl.BlockSpec((tm, KV_WIDTH), lambda i: (i, 0)),
        ],
        out_shape=[
            jax.ShapeDtypeStruct((rows, Q_WIDTH), BF16),
            jax.ShapeDtypeStruct((rows, KV_WIDTH), F32),
            jax.ShapeDtypeStruct((rows, KV_WIDTH), F32),
        ],
        compiler_params=_params(1),
        name="attn_proj",
    )(x, mods_l, gain.reshape(1, D_MODEL), w_bf, q_gain.reshape(1, HEAD_DIM), k_gain.reshape(1, HEAD_DIM),
      rope_cos, rope_sin)


def _attn_kernel(*refs, n_pieces):
    q_ref = refs[0]
    kv_refs = refs[1:1 + 2 * n_pieces]
    o_ref = refs[1 + 2 * n_pieces]
    q = q_ref[...]
    for kv in range(KV_HEADS):
        sl = slice(kv * HEAD_DIM, (kv + 1) * HEAD_DIM)
        ks = [kv_refs[2 * p][:, sl].astype(BF16) for p in range(n_pieces)]
        vs = [kv_refs[2 * p + 1][:, sl].astype(BF16) for p in range(n_pieces)]
        for g in range(GROUP):
            hh = kv * GROUP + g
            qh = q[:, hh * HEAD_DIM:(hh + 1) * HEAD_DIM]
            ss = [_dot_nt(qh, k) for k in ks]
            m = functools.reduce(jnp.maximum, [jnp.max(s, axis=-1, keepdims=True) for s in ss])
            ps = [jnp.exp(s - m) for s in ss]
            denom = functools.reduce(jnp.add, [jnp.sum(p, axis=-1, keepdims=True) for p in ps])
            o = functools.reduce(jnp.add, [_dot(p.astype(BF16), v) for p, v in zip(ps, vs)])
            o_ref[:, hh * HEAD_DIM:(hh + 1) * HEAD_DIM] = (o / denom).astype(BF16)


def _context_attention(q, k, v, seq):
    rows = q.shape[0]
    return pl.pallas_call(
        functools.partial(_attn_kernel, n_pieces=1),
        grid=(rows // seq,),
        in_specs=[
            pl.BlockSpec((seq, Q_WIDTH), lambda b: (b, 0)),
            pl.BlockSpec((seq, KV_WIDTH), lambda b: (b, 0)),
            pl.BlockSpec((seq, KV_WIDTH), lambda b: (b, 0)),
        ],
        out_specs=pl.BlockSpec((seq, Q_WIDTH), lambda b: (b, 0)),
        out_shape=jax.ShapeDtypeStruct((rows, Q_WIDTH), BF16),
        compiler_params=_params(1),
        name="context_attention",
    )(q, k, v)


def _latent_attention(q, k, v, cache_k4, cache_v4, layer_j, seq, tq):
    rows = q.shape[0]
    past = cache_k4.shape[2]
    nq = seq // tq
    cache_spec = pl.BlockSpec((None, None, past, KV_WIDTH), lambda b, i: (b, layer_j, 0, 0))
    return pl.pallas_call(
        functools.partial(_attn_kernel, n_pieces=2),
        grid=(rows // seq, nq),
        in_specs=[
            pl.BlockSpec((tq, Q_WIDTH), lambda b, i: (b * nq + i, 0)),
            cache_spec,
            cache_spec,
            pl.BlockSpec((seq, KV_WIDTH), lambda b, i: (b, 0)),
            pl.BlockSpec((seq, KV_WIDTH), lambda b, i: (b, 0)),
        ],
        out_specs=pl.BlockSpec((tq, Q_WIDTH), lambda b, i: (b * nq + i, 0)),
        out_shape=jax.ShapeDtypeStruct((rows, Q_WIDTH), BF16),
        compiler_params=_params(2),
        name="latent_attention",
    )(q, cache_k4, cache_v4, k, v)


def _dn_proj_kernel(x_ref, mods_ref, gain_ref, w_ref, wab_ref, alog_ref, dtb_ref, main_out, gb_out):
    x = x_ref[...]
    h = (_rms(x, gain_ref[...]) * (1.0 + mods_ref[1:2, :]) + mods_ref[0:1, :]).astype(BF16)
    main_out[...] = _dot(h, w_ref[...])
    ab = _dot(h, wab_ref[...])
    lane = lax.broadcasted_iota(jnp.int32, ab.shape, 1)
    g = -jnp.exp(alog_ref[...]) * jax.nn.softplus(ab + dtb_ref[...])
    gb_out[...] = jnp.where(lane < 2 * DN_HEADS, g, jax.nn.sigmoid(ab))


def _dn_proj(x, mods_l, cond_of_tile, gain, w_main_bf, w_ab_bf, alog_row, dtb_row, tm):
    rows = x.shape[0]
    return pl.pallas_call(
        _dn_proj_kernel,
        grid=(rows // tm,),
        in_specs=[
            pl.BlockSpec((tm, D_MODEL), lambda i: (i, 0)),
            _mods_spec(cond_of_tile),
            _const_spec((1, D_MODEL)),
            _const_spec((D_MODEL, DN_MAIN)),
            _const_spec((D_MODEL, LANES)),
            _const_spec((1, LANES)),
            _const_spec((1, LANES)),
        ],
        out_specs=[
            pl.BlockSpec((tm, DN_MAIN), lambda i: (i, 0)),
            pl.BlockSpec((tm, LANES), lambda i: (i, 0)),
        ],
        out_shape=[
            jax.ShapeDtypeStruct((rows, DN_MAIN), F32),
            jax.ShapeDtypeStruct((rows, LANES), F32),
        ],
        compiler_params=_params(1),
        name="dn_proj",
    )(x, mods_l, gain.reshape(1, D_MODEL), w_main_bf, w_ab_bf, alog_row, dtb_row)


def _split_bf16(a):
    hi = a.astype(BF16)
    lo = (a - hi.astype(F32)).astype(BF16)
    return hi, lo


def _dot_hi(a, b):
    ah, al = _split_bf16(a)
    bh, bl = _split_bf16(b)
    return _dot(ah, bh) + (_dot(ah, bl) + _dot(al, bh))


def _unit_tri_inverse(low, eye):
    p = eye - low
    m = low
    steps = CHUNK.bit_length() - 2
    for _ in range(steps):
        m = _dot_hi(m, m)
        p = p + _dot_hi(p, m)
    return p


def _delta_kernel(*refs, seq, has_s0):
    (q_ref, k_ref, v_ref, z_ref, gb_ref, gt_ref, cq_ref, ck_ref, cv_ref, ng_ref) = refs[:10]
    pos = 10
    s0_ref = None
    if has_s0:
        s0_ref = refs[pos]
        pos += 1
    og_ref = refs[pos]
    pos += 1
    st_ref = None
    if not has_s0:
        st_ref = refs[pos]
        pos += 1
    q_s, k_s, v_s, of_s, ob_s, pad_s = refs[pos:]

    head = pl.program_id(1)
    n_chunks = seq // CHUNK
    halo = 8

    pad_s[0:halo, :] = jnp.zeros((halo, DN_DK), F32)
    pad_s[halo + seq:2 * halo + seq, :] = jnp.zeros((halo, DN_DK), F32)

    def conv_silu(x_ref, cw_ref):
        pad_s[halo:halo + seq, :] = x_ref[...]
        base = halo - CONV_K // 2
        acc = pad_s[base:base + seq, :] * cw_ref[0:1, :]
        for j in range(1, CONV_K):
            acc = acc + pad_s[base + j:base + j + seq, :] * cw_ref[j:j + 1, :]
        return _silu(acc)

    def l2n(x):
        return x * lax.rsqrt(jnp.sum(x * x, axis=-1, keepdims=True) + EPS)

    q_s[...] = l2n(conv_silu(q_ref, cq_ref)) * (DN_DK ** -0.5)
    k_s[...] = l2n(conv_silu(k_ref, ck_ref))
    v_s[...] = conv_silu(v_ref, cv_ref)

    row_i = lax.broadcasted_iota(jnp.int32, (CHUNK, CHUNK), 0)
    col_j = lax.broadcasted_iota(jnp.int32, (CHUNK, CHUNK), 1)
    eye = (row_i == col_j).astype(F32)
    lane = lax.broadcasted_iota(jnp.int32, (CHUNK, LANES), 1)

    def chunk_step(n, state, direction):
        r0 = pl.multiple_of(n * CHUNK, CHUNK)
        qc = q_s[pl.ds(r0, CHUNK), :]
        kc = k_s[pl.ds(r0, CHUNK), :]
        vc = v_s[pl.ds(r0, CHUNK), :]
        gbc = gb_ref[pl.ds(r0, CHUNK), :]
        col = direction * DN_HEADS + head
        g_col = jnp.sum(jnp.where(lane == col, gbc, 0.0), axis=1, keepdims=True)
        b_col = jnp.sum(jnp.where(lane == col + 2 * DN_HEADS, gbc, 0.0), axis=1, keepdims=True)
        g_row = gt_ref[n, pl.ds(col, 1), :]
        if direction == 0:
            incl = col_j <= row_i
            strict = col_j < row_i
            incl_t = row_i <= col_j
        else:
            incl = col_j >= row_i
            strict = col_j > row_i
            incl_t = row_i >= col_j
        gc_col = jnp.sum(jnp.where(incl, g_row, 0.0), axis=1, keepdims=True)
        gc_row = jnp.sum(jnp.where(incl_t, g_col, 0.0), axis=0, keepdims=True)
        g_tot = jnp.sum(g_row, axis=1, keepdims=True)
        decay = jnp.exp(jnp.where(incl, gc_col - gc_row, 0.0))
        kb = kc * b_col
        vb = vc * b_col
        kc_bf = kc.astype(BF16)
        low = jnp.where(strict, _dot_nt(kb.astype(BF16), kc_bf) * decay, 0.0)
        qk = jnp.where(incl, _dot_nt(qc.astype(BF16), kc_bf) * decay, 0.0)
        tinv = _unit_tri_inverse(low, eye)
        eg = jnp.exp(gc_col)
        rhs = jnp.concatenate([vb, kb * eg], axis=1).astype(BF16)
        uw = _dot(tinv.astype(BF16), rhs)
        u = uw[:, :DN_DK]
        w = uw[:, DN_DK:]
        s_bf = state.astype(BF16)
        v_new = u - _dot(w.astype(BF16), s_bf)
        v_new_bf = v_new.astype(BF16)
        o = _dot((qc * eg).astype(BF16), s_bf) + _dot(qk.astype(BF16), v_new_bf)
        kd_t = jnp.transpose(kc * jnp.exp(g_tot - gc_col)).astype(BF16)
        new_state = state * jnp.exp(g_tot) + _dot(kd_t, v_new_bf)
        return new_state, o, r0

    def body(n, carry):
        s_f, s_b = carry
        s_f, o_f, r_f = chunk_step(n, s_f, 0)
        of_s[pl.ds(r_f, CHUNK), :] = o_f
        s_b, o_b, r_b = chunk_step(n_chunks - 1 - n, s_b, 1)
        ob_s[pl.ds(r_b, CHUNK), :] = o_b
        return s_f, s_b

    if has_s0:
        init = (s0_ref[0], s0_ref[1])
    else:
        init = (jnp.zeros((DN_DK, DN_DK), F32), jnp.zeros((DN_DK, DN_DK), F32))
    s_f, s_b = lax.fori_loop(0, n_chunks, body, init)
    if st_ref is not None:
        st_ref[0] = s_f
        st_ref[1] = s_b

    o = of_s[...] + ob_s[...]
    og_ref[...] = (_rms(o, ng_ref[...]) * _silu(z_ref[...])).astype(BF16)


def _delta_core(main, gb, gt, conv_w, norm_gain, s0, layer_j, seq):
    rows = main.shape[0]
    n_seq = rows // seq
    n_chunks = seq // CHUNK
    has_s0 = s0 is not None

    def col_spec(group):
        return pl.BlockSpec((seq, DN_DK), lambda b, h: (b, group * DN_HEADS + h))

    def conv_spec(group):
        return pl.BlockSpec((CONV_K, DN_DK), lambda b, h: (0, group * DN_HEADS + h))

    in_specs = [
        col_spec(0), col_spec(1), col_spec(2), col_spec(3),
        pl.BlockSpec((seq, LANES), lambda b, h: (b, 0)),
        pl.BlockSpec((n_chunks, 4 * DN_HEADS, CHUNK), lambda b, h: (b, 0, 0)),
        conv_spec(0), conv_spec(1), conv_spec(2),
        pl.BlockSpec((1, DN_DK), lambda b, h: (0, 0)),
    ]
    args = [main, main, main, main, gb, gt, conv_w, conv_w, conv_w, norm_gain.reshape(1, DN_DK)]
    out_specs = [pl.BlockSpec((seq, DN_DK), lambda b, h: (b, h))]
    out_shape = [jax.ShapeDtypeStruct((rows, DN_KEY), BF16)]
    if has_s0:
        in_specs.append(pl.BlockSpec((None, None, 2, None, DN_DK, DN_DK), lambda b, h: (b, layer_j, 0, h, 0, 0)))
        args.append(s0)
    else:
        out_specs.append(pl.BlockSpec((None, 2, None, DN_DK, DN_DK), lambda b, h: (b, 0, h, 0, 0)))
        out_shape.append(jax.ShapeDtypeStruct((n_seq, 2, DN_HEADS, DN_DK, DN_DK), F32))
    scratch = [pltpu.VMEM((seq, DN_DK), F32) for _ in range(5)]
    scratch.append(pltpu.VMEM((seq + 16, DN_DK), F32))
    return pl.pallas_call(
        functools.partial(_delta_kernel, seq=seq, has_s0=has_s0),
        grid=(n_seq, DN_HEADS),
        in_specs=in_specs,
        out_specs=out_specs,
        out_shape=out_shape,
        scratch_shapes=scratch,
        compiler_params=_params(2),
        name="delta_core",
    )(*args)


def _post_kernel(*refs, final, ff_chunk):
    x_ref, o_ref, mods_ref, g2_ref, wout_ref, w1_ref, w2_ref = refs[:7]
    fin_ref = refs[7] if final else None
    out_ref = refs[-1]
    x1 = x_ref[...] + mods_ref[2:3, :] * _dot(o_ref[...], wout_ref[...])
    h = (_rms(x1, g2_ref[...]) * (1.0 + mods_ref[4:5, :]) + mods_ref[3:4, :]).astype(BF16)
    acc = jnp.zeros(x1.shape, F32)
    for c in range(D_FF // ff_chunk):
        a = _dot(h, w1_ref[:, c * ff_chunk:(c + 1) * ff_chunk])
        a = jnp.square(jnp.maximum(a, 0.0)).astype(BF16)
        acc = acc + _dot(a, w2_ref[c * ff_chunk:(c + 1) * ff_chunk, :])
    x2 = x1 + mods_ref[5:6, :] * acc
    if final:
        x2 = _rms(x2, fin_ref[...])
    out_ref[...] = x2


def _post(x, o, mods_l, cond_of_tile, gain2, wout_bf, w1_bf, w2_bf, final_gain, tm):
    rows = x.shape[0]
    final = final_gain is not None
    in_specs = [
        pl.BlockSpec((tm, D_MODEL), lambda i: (i, 0)),
        pl.BlockSpec((tm, D_MODEL), lambda i: (i, 0)),
        _mods_spec(cond_of_tile),
        _const_spec((1, D_MODEL)),
        _const_spec((D_MODEL, D_MODEL)),
        _const_spec((D_MODEL, D_FF)),
        _const_spec((D_FF, D_MODEL)),
    ]
    args = [x, o, mods_l, gain2.reshape(1, D_MODEL), wout_bf, w1_bf, w2_bf]
    if final:
        in_specs.append(_const_spec((1, D_MODEL)))
        args.append(final_gain.reshape(1, D_MODEL))
    return pl.pallas_call(
        functools.partial(_post_kernel, final=final, ff_chunk=1024),
        grid=(rows // tm,),
        in_specs=in_specs,
        out_specs=pl.BlockSpec((tm, D_MODEL), lambda i: (i, 0)),
        out_shape=jax.ShapeDtypeStruct((rows, D_MODEL), F32),
        compiler_params=_params(1),
        name="post_mlp",
    )(*args)


def _rope_tables(n_tokens):
    rows = n_tokens // GRID_W
    row = jnp.repeat(jnp.arange(rows), GRID_W).astype(F32)
    col = jnp.tile(jnp.arange(GRID_W), rows).astype(F32)
    inv = ROPE_THETA ** (-jnp.arange(ROPE_FREQS, dtype=F32) / ROPE_FREQS)
    ang_r = row[:, None] * inv
    ang_c = col[:, None] * inv
    cos = jnp.concatenate([jnp.cos(ang_r), jnp.cos(ang_r), jnp.cos(ang_c), jnp.cos(ang_c)], axis=1)
    sin = jnp.concatenate([-jnp.sin(ang_r), jnp.sin(ang_r), -jnp.sin(ang_c), jnp.sin(ang_c)], axis=1)
    cos = jnp.concatenate([jnp.ones_like(cos), cos], axis=0)
    sin = jnp.concatenate([jnp.zeros_like(sin), sin], axis=0)
    return cos, sin


def kernel(x_prompt, x_sample, cache_k, cache_v, state_delta, c, c_ctx, w_mod, b_mod, norm1, norm2,
           attn_w_in, attn_q_gain, attn_k_gain, attn_w_out, dn_w_in, dn_conv, dn_a_log, dn_dt_bias,
           dn_norm, dn_w_out, mlp_w1, mlp_w2, final_norm):
    n_ctx, seq_ctx, _ = x_prompt.shape
    n_lat, seq_lat, _ = x_sample.shape
    past = cache_k.shape[2]
    tm = 256

    cond = jnp.concatenate([c_ctx[None, :], c, jnp.zeros((N_COND - 1 - n_lat, D_MODEL), F32)], axis=0)
    mods = _modulation(cond, w_mod, b_mod).reshape(DEPTH, N_COND, N_MOD, D_MODEL)

    rope_cos, rope_sin = _rope_tables(seq_lat)
    lat_tiles = seq_lat // tm
    cache_k4 = cache_k.reshape(n_lat, cache_k.shape[1], past, KV_WIDTH)
    cache_v4 = cache_v.reshape(n_lat, cache_v.shape[1], past, KV_WIDTH)

    streams = {
        "ctx": dict(x=x_prompt.reshape(n_ctx * seq_ctx, D_MODEL), seq=seq_ctx,
                    cond=lambda i: 0, rope=lambda i: 0),
        "lat": dict(x=x_sample.reshape(n_lat * seq_lat, D_MODEL), seq=seq_lat,
                    cond=lambda i: 1 + i // lat_tiles, rope=lambda i: lat_tiles + i % lat_tiles),
    }
    keys_out, vals_out, states_out = [], [], []

    for l in range(DEPTH):
        j = l // 2
        mods_l = mods[l]
        w1_bf = mlp_w1[l].astype(BF16)
        w2_bf = mlp_w2[l].astype(BF16)
        final_gain = final_norm if l == DEPTH - 1 else None
        if l % 2 == 0:
            w_in_bf = attn_w_in[j].astype(BF16)
            w_out_bf = attn_w_out[j].astype(BF16)
        else:
            w_main_bf = dn_w_in[j][:, :DN_MAIN].astype(BF16)
            w_ab_bf = jnp.pad(dn_w_in[j][:, DN_MAIN:], ((0, 0), (0, LANES - 4 * DN_HEADS))).astype(BF16)
            w_out_bf = dn_w_out[j].astype(BF16)
            alog_row = jnp.pad(dn_a_log[j].reshape(1, 2 * DN_HEADS), ((0, 0), (0, LANES - 2 * DN_HEADS)))
            dtb_row = jnp.pad(dn_dt_bias[j].reshape(1, 2 * DN_HEADS), ((0, 0), (0, LANES - 2 * DN_HEADS)))
        for name, st in streams.items():
            x = st["x"]
            seq = st["seq"]
            rows = x.shape[0]
            if l % 2 == 0:
                q, k, v = _attn_proj(x, mods_l, st["cond"], norm1[l], w_in_bf, attn_q_gain[j], attn_k_gain[j],
                                     rope_cos, rope_sin, st["rope"], tm)
                if name == "ctx":
                    keys_out.append(k.reshape(n_ctx, seq, KV_HEADS, HEAD_DIM))
                    vals_out.append(v.reshape(n_ctx, seq, KV_HEADS, HEAD_DIM))
                    o = _context_attention(q, k, v, seq)
                else:
                    o = _latent_attention(q, k, v, cache_k4, cache_v4, j, seq, tm)
            else:
                main, gb = _dn_proj(x, mods_l, st["cond"], norm1[l], w_main_bf, w_ab_bf, alog_row, dtb_row, tm)
                gt = gb[:, :4 * DN_HEADS].reshape(rows // CHUNK, CHUNK, 4 * DN_HEADS).transpose(0, 2, 1)
                if name == "ctx":
                    o, states = _delta_core(main, gb, gt, dn_conv[j], dn_norm[j], None, j, seq)
                    states_out.append(states)
                else:
                    (o,) = _delta_core(main, gb, gt, dn_conv[j], dn_norm[j], state_delta, j, seq)
            st["x"] = _post(x, o, mods_l, st["cond"], norm2[l], w_out_bf, w1_bf, w2_bf, final_gain, tm)

    y_prompt = streams["ctx"]["x"].reshape(n_ctx, seq_ctx, D_MODEL)
    y_sample = streams["lat"]["x"].reshape(n_lat, seq_lat, D_MODEL)
    new_cache_k = jnp.stack(keys_out, axis=1)
    new_cache_v = jnp.stack(vals_out, axis=1)
    new_state_delta = jnp.stack(states_out, axis=1)
    return (y_prompt, y_sample, new_cache_k, new_cache_v, new_state_delta)
```

```python
import functools

import jax
import jax.numpy as jnp
from jax import lax
from jax.experimental import pallas as pl
from jax.experimental.pallas import tpu as pltpu

F32 = jnp.float32
BF16 = jnp.bfloat16

D_MODEL = 1024
DEPTH = 4
GRID_W = 64
HEAD_DIM = 128
N_HEADS = 8
KV_HEADS = 2
GROUP = N_HEADS // KV_HEADS
Q_WIDTH = N_HEADS * HEAD_DIM
KV_WIDTH = KV_HEADS * HEAD_DIM
ROPE_THETA = 10000.0
ROPE_FREQS = HEAD_DIM // 4
DN_HEADS = 8
DN_DK = 128
DN_KEY = DN_HEADS * DN_DK
DN_MAIN = 4 * DN_KEY
CONV_K = 5
CHUNK = 64
HI_STEPS = 6
D_FF = 4 * D_MODEL
N_MOD = 6
EPS = 1e-6
N_COND = 8
LANES = 128
VMEM_LIMIT = 52 * 1024 * 1024


def _params(n_grid):
    return pltpu.CompilerParams(dimension_semantics=("arbitrary",) * n_grid,
                                vmem_limit_bytes=VMEM_LIMIT)


def _const_spec(shape):
    nd = len(shape)
    return pl.BlockSpec(shape, lambda *_: (0,) * nd)


def _dot(a, b):
    return jnp.dot(a, b, preferred_element_type=F32)


def _dot_nt(a, b):
    return lax.dot_general(a, b, (((1,), (1,)), ((), ())), preferred_element_type=F32)


def _silu(x):
    return x * jax.nn.sigmoid(x)


def _rms(x, gain):
    return x * lax.rsqrt(jnp.mean(x * x, axis=-1, keepdims=True) + EPS) * gain


def _mod_kernel(cond_ref, w_ref, b_ref, o_ref):
    s = _silu(cond_ref[...]).astype(BF16)
    o_ref[...] = _dot(s, w_ref[...].astype(BF16)) + b_ref[...]


def _modulation(cond, w_mod, b_mod):
    tn = 1536
    n_out = N_MOD * D_MODEL
    return pl.pallas_call(
        _mod_kernel,
        grid=(DEPTH, n_out // tn),
        in_specs=[
            _const_spec((N_COND, D_MODEL)),
            pl.BlockSpec((None, D_MODEL, tn), lambda l, n: (l, 0, n)),
            pl.BlockSpec((None, 1, tn), lambda l, n: (l, 0, n)),
        ],
        out_specs=pl.BlockSpec((None, N_COND, tn), lambda l, n: (l, 0, n)),
        out_shape=jax.ShapeDtypeStruct((DEPTH, N_COND, n_out), F32),
        compiler_params=_params(2),
        name="modulation",
    )(cond, w_mod, b_mod.reshape(DEPTH, 1, n_out))


def _mods_spec(cond_of_tile):
    return pl.BlockSpec((None, N_MOD, D_MODEL), lambda i: (cond_of_tile(i), 0, 0))


def _attn_proj_kernel(x_ref, mods_ref, gain_ref, w_ref, qg_ref, kg_ref, cos_ref, sin_ref,
                      q_out, k_out, v_out):
    x = x_ref[...]
    h = _rms(x, gain_ref[...]) * (1.0 + mods_ref[1:2, :]) + mods_ref[0:1, :]
    proj = _dot(h.astype(BF16), w_ref[...])
    cos = cos_ref[...]
    sin = sin_ref[...]
    lane = lax.broadcasted_iota(jnp.int32, cos.shape, 1)
    low_half = (lane % (2 * ROPE_FREQS)) < ROPE_FREQS

    def prep(xh, gain):
        y = _rms(xh, gain)
        partner = jnp.where(low_half, pltpu.roll(y, LANES - ROPE_FREQS, 1), pltpu.roll(y, ROPE_FREQS, 1))
        return y * cos + partner * sin

    for hh in range(N_HEADS):
        sl = slice(hh * HEAD_DIM, (hh + 1) * HEAD_DIM)
        q_out[:, sl] = (prep(proj[:, sl], qg_ref[...]) * (HEAD_DIM ** -0.5)).astype(BF16)
    for kv in range(KV_HEADS):
        sl = slice(kv * HEAD_DIM, (kv + 1) * HEAD_DIM)
        k_out[:, sl] = prep(proj[:, Q_WIDTH + kv * HEAD_DIM:Q_WIDTH + (kv + 1) * HEAD_DIM], kg_ref[...])
    v_out[...] = proj[:, Q_WIDTH + KV_WIDTH:]


def _attn_proj(x, mods_l, cond_of_tile, gain, w_bf, q_gain, k_gain, rope_cos, rope_sin, rope_of_tile, tm):
    rows = x.shape[0]
    return pl.pallas_call(
        _attn_proj_kernel,
        grid=(rows // tm,),
        in_specs=[
            pl.BlockSpec((tm, D_MODEL), lambda i: (i, 0)),
            _mods_spec(cond_of_tile),
            _const_spec((1, D_MODEL)),
            _const_spec((D_MODEL, Q_WIDTH + 2 * KV_WIDTH)),
            _const_spec((1, HEAD_DIM)),
            _const_spec((1, HEAD_DIM)),
            pl.BlockSpec((tm, HEAD_DIM), lambda i: (rope_of_tile(i), 0)),
            pl.BlockSpec((tm, HEAD_DIM), lambda i: (rope_of_tile(i), 0)),
        ],
        out_specs=[
            pl.BlockSpec((tm, Q_WIDTH), lambda i: (i, 0)),
            pl.BlockSpec((tm, KV_WIDTH), lambda i: (i, 0)),
            pl.BlockSpec((tm, KV_WIDTH), lambda i: (i, 0)),
        ],
        out_shape=[
            jax.ShapeDtypeStruct((rows, Q_WIDTH), BF16),
            jax.ShapeDtypeStruct((rows, KV_WIDTH), F32),
            jax.ShapeDtypeStruct((rows, KV_WIDTH), F32),
        ],
        compiler_params=_params(1),
        name="attn_proj",
    )(x, mods_l, gain.reshape(1, D_MODEL), w_bf, q_gain.reshape(1, HEAD_DIM), k_gain.reshape(1, HEAD_DIM),
      rope_cos, rope_sin)


def _attn_kernel(*refs, n_pieces):
    q_ref = refs[0]
    kv_refs = refs[1:1 + 2 * n_pieces]
    o_ref = refs[1 + 2 * n_pieces]
    q = q_ref[...]
    for kv in range(KV_HEADS):
        sl = slice(kv * HEAD_DIM, (kv + 1) * HEAD_DIM)
        ks = [kv_refs[2 * p][:, sl].astype(BF16) for p in range(n_pieces)]
        vs = [kv_refs[2 * p + 1][:, sl].astype(BF16) for p in range(n_pieces)]
        for g in range(GROUP):
            hh = kv * GROUP + g
            qh = q[:, hh * HEAD_DIM:(hh + 1) * HEAD_DIM]
            ss = [_dot_nt(qh, k) for k in ks]
            m = functools.reduce(jnp.maximum, [jnp.max(s, axis=-1, keepdims=True) for s in ss])
            ps = [jnp.exp(s - m) for s in ss]
            denom = functools.reduce(jnp.add, [jnp.sum(p, axis=-1, keepdims=True) for p in ps])
            o = functools.reduce(jnp.add, [_dot(p.astype(BF16), v) for p, v in zip(ps, vs)])
            o_ref[:, hh * HEAD_DIM:(hh + 1) * HEAD_DIM] = (o / denom).astype(BF16)


def _context_attention(q, k, v, seq):
    rows = q.shape[0]
    return pl.pallas_call(
        functools.partial(_attn_kernel, n_pieces=1),
        grid=(rows // seq,),
        in_specs=[
            pl.BlockSpec((seq, Q_WIDTH), lambda b: (b, 0)),
            pl.BlockSpec((seq, KV_WIDTH), lambda b: (b, 0)),
            pl.BlockSpec((seq, KV_WIDTH), lambda b: (b, 0)),
        ],
        out_specs=pl.BlockSpec((seq, Q_WIDTH), lambda b: (b, 0)),
        out_shape=jax.ShapeDtypeStruct((rows, Q_WIDTH), BF16),
        compiler_params=_params(1),
        name="context_attention",
    )(q, k, v)


def _latent_attention(q, k, v, cache_k4, cache_v4, layer_j, seq, tq):
    rows = q.shape[0]
    past = cache_k4.shape[2]
    nq = seq // tq
    cache_spec = pl.BlockSpec((None, None, past, KV_WIDTH), lambda b, i: (b, layer_j, 0, 0))
    return pl.pallas_call(
        functools.partial(_attn_kernel, n_pieces=2),
        grid=(rows // seq, nq),
        in_specs=[
            pl.BlockSpec((tq, Q_WIDTH), lambda b, i: (b * nq + i, 0)),
            cache_spec,
            cache_spec,
            pl.BlockSpec((seq, KV_WIDTH), lambda b, i: (b, 0)),
            pl.BlockSpec((seq, KV_WIDTH), lambda b, i: (b, 0)),
        ],
        out_specs=pl.BlockSpec((tq, Q_WIDTH), lambda b, i: (b * nq + i, 0)),
        out_shape=jax.ShapeDtypeStruct((rows, Q_WIDTH), BF16),
        compiler_params=_params(2),
        name="latent_attention",
    )(q, cache_k4, cache_v4, k, v)


def _dn_proj_kernel(x_ref, mods_ref, gain_ref, w_ref, wab_ref, alog_ref, dtb_ref, main_out, gb_out):
    x = x_ref[...]
    h = (_rms(x, gain_ref[...]) * (1.0 + mods_ref[1:2, :]) + mods_ref[0:1, :]).astype(BF16)
    main_out[...] = _dot(h, w_ref[...])
    ab = _dot(h, wab_ref[...])
    lane = lax.broadcasted_iota(jnp.int32, ab.shape, 1)
    g = -jnp.exp(alog_ref[...]) * jax.nn.softplus(ab + dtb_ref[...])
    tm = ab.shape[0]
    r = lax.broadcasted_iota(jnp.int32, (tm, tm), 0)
    c = lax.broadcasted_iota(jnp.int32, (tm, tm), 1)
    same_chunk = (r // CHUNK) == (c // CHUNK)
    m_fwd = jnp.where(same_chunk, jnp.where(c <= r, 1.0, 0.0), 0.0).astype(BF16)
    m_bwd = jnp.where(same_chunk, jnp.where(c >= r, 1.0, 0.0), 0.0).astype(BF16)
    g1 = g.astype(BF16)
    rem = g - g1.astype(F32)
    g2 = rem.astype(BF16)
    g3 = (rem - g2.astype(F32)).astype(BF16)
    cum_f = _dot(m_fwd, g1) + (_dot(m_fwd, g2) + _dot(m_fwd, g3))
    cum_b = _dot(m_bwd, g1) + (_dot(m_bwd, g2) + _dot(m_bwd, g3))
    total = cum_f + cum_b - g
    out = jnp.where(lane < DN_HEADS, cum_f, jnp.where(lane < 2 * DN_HEADS, cum_b, jax.nn.sigmoid(ab)))
    gb_out[...] = jnp.where(lane < 4 * DN_HEADS, out, pltpu.roll(total, 4 * DN_HEADS, 1))


def _dn_proj(x, mods_l, cond_of_tile, gain, w_main_bf, w_ab_bf, alog_row, dtb_row, tm):
    rows = x.shape[0]
    return pl.pallas_call(
        _dn_proj_kernel,
        grid=(rows // tm,),
        in_specs=[
            pl.BlockSpec((tm, D_MODEL), lambda i: (i, 0)),
            _mods_spec(cond_of_tile),
            _const_spec((1, D_MODEL)),
            _const_spec((D_MODEL, DN_MAIN)),
            _const_spec((D_MODEL, LANES)),
            _const_spec((1, LANES)),
            _const_spec((1, LANES)),
        ],
        out_specs=[
            pl.BlockSpec((tm, DN_MAIN), lambda i: (i, 0)),
            pl.BlockSpec((tm, LANES), lambda i: (i, 0)),
        ],
        out_shape=[
            jax.ShapeDtypeStruct((rows, DN_MAIN), F32),
            jax.ShapeDtypeStruct((rows, LANES), F32),
        ],
        compiler_params=_params(1),
        name="dn_proj",
    )(x, mods_l, gain.reshape(1, D_MODEL), w_main_bf, w_ab_bf, alog_row, dtb_row)


def _split_bf16(a):
    hi = a.astype(BF16)
    lo = (a - hi.astype(F32)).astype(BF16)
    return hi, lo


def _unit_tri_inverse(low, eye):
    c = CHUNK
    p = eye - low
    m = low
    steps = CHUNK.bit_length() - 2
    for it in range(steps + 1):
        need_square = it < steps
        need_pm = it > 0
        if it < HI_STEPS:
            mh, ml = _split_bf16(m)
            lhs_a, lhs_b = [], []
            if need_square:
                lhs_a += [mh, ml]
                lhs_b += [mh]
            if need_pm:
                ph, pl_ = _split_bf16(p)
                lhs_a += [ph, pl_]
                lhs_b += [ph]
            a = _dot(jnp.concatenate(lhs_a, axis=0), mh)
            b = _dot(jnp.concatenate(lhs_b, axis=0), ml)
            off_a = off_b = 0
            if need_square:
                m_next = a[0:c] + a[c:2 * c] + b[0:c]
                off_a, off_b = 2 * c, c
            if need_pm:
                p = p + (a[off_a:off_a + c] + a[off_a + c:off_a + 2 * c] + b[off_b:off_b + c])
        else:
            mb = m.astype(BF16)
            lhs = []
            if need_square:
                lhs.append(mb)
            if need_pm:
                lhs.append(p.astype(BF16))
            a = _dot(jnp.concatenate(lhs, axis=0), mb)
            off = 0
            if need_square:
                m_next = a[0:c]
                off = c
            if need_pm:
                p = p + a[off:off + c]
        if need_square:
            m = m_next
        yield
    return p


def _delta_kernel(*refs, seq, hg, has_s0):
    (q_ref, k_ref, v_ref, z_ref, gb_ref, gt_ref, cq_ref, ck_ref, cv_ref, ng_ref) = refs[:10]
    pos = 10
    s0_ref = None
    if has_s0:
        s0_ref = refs[pos]
        pos += 1
    og_ref = refs[pos]
    pos += 1
    st_ref = None
    if not has_s0:
        st_ref = refs[pos]
        pos += 1
    q_s, k_s, v_s, of_s, ob_s, pad_s, st_s = refs[pos:]

    head0 = pl.program_id(1) * hg
    n_chunks = seq // CHUNK
    halo = 8
    width = hg * DN_DK

    pad_s[0:halo, :] = jnp.zeros((halo, width), F32)
    pad_s[halo + seq:2 * halo + seq, :] = jnp.zeros((halo, width), F32)

    def conv_silu(x_ref, cw_ref):
        pad_s[halo:halo + seq, :] = x_ref[...]
        base = halo - CONV_K // 2
        acc = pad_s[base:base + seq, :] * cw_ref[0:1, :]
        for j in range(1, CONV_K):
            acc = acc + pad_s[base + j:base + j + seq, :] * cw_ref[j:j + 1, :]
        return _silu(acc)

    def l2n_heads(x, scale, dst):
        for h in range(hg):
            sl = slice(h * DN_DK, (h + 1) * DN_DK)
            xh = x[:, sl]
            dst[:, sl] = xh * (lax.rsqrt(jnp.sum(xh * xh, axis=-1, keepdims=True) + EPS) * scale)

    l2n_heads(conv_silu(q_ref, cq_ref), DN_DK ** -0.5, q_s)
    l2n_heads(conv_silu(k_ref, ck_ref), 1.0, k_s)
    v_s[...] = conv_silu(v_ref, cv_ref)

    for d in range(2):
        for h in range(hg):
            if has_s0:
                st_s[d * hg + h] = s0_ref[d, h]
            else:
                st_s[d * hg + h] = jnp.zeros((DN_DK, DN_DK), F32)

    row_i = lax.broadcasted_iota(jnp.int32, (CHUNK, CHUNK), 0)
    col_j = lax.broadcasted_iota(jnp.int32, (CHUNK, CHUNK), 1)
    eye = (row_i == col_j).astype(F32)
    lane = lax.broadcasted_iota(jnp.int32, (CHUNK, LANES), 1)

    def chain_step(n, h, direction):
        r0 = pl.multiple_of(n * CHUNK, CHUNK)
        sl = slice(h * DN_DK, (h + 1) * DN_DK)
        qc = q_s[pl.ds(r0, CHUNK), sl]
        kc = k_s[pl.ds(r0, CHUNK), sl]
        vc = v_s[pl.ds(r0, CHUNK), sl]
        gbc = gb_ref[pl.ds(r0, CHUNK), :]
        col = direction * DN_HEADS + head0 + h
        gc_col = jnp.sum(jnp.where(lane == col, gbc, 0.0), axis=1, keepdims=True)
        b_col = jnp.sum(jnp.where(lane == col + 2 * DN_HEADS, gbc, 0.0), axis=1, keepdims=True)
        gc_row = gt_ref[n, pl.ds(col, 1), :]
        b_row = gt_ref[n, pl.ds(col + 2 * DN_HEADS, 1), :]
        g_tot = gt_ref[n, pl.ds(col + 4 * DN_HEADS, 1), :][:, 0:1]
        if direction == 0:
            incl = col_j <= row_i
            strict = col_j < row_i
        else:
            incl = col_j >= row_i
            strict = col_j > row_i
        decay = jnp.exp(jnp.where(incl, gc_col - gc_row, 0.0))
        kc_bf = kc.astype(BF16)
        kq = _dot_nt(jnp.concatenate([kc_bf, qc.astype(BF16)], axis=0), kc_bf)
        yield
        low = jnp.where(strict, kq[0:CHUNK] * (decay * b_col), 0.0)
        qk = jnp.where(incl, kq[CHUNK:2 * CHUNK] * decay, 0.0)
        tinv = yield from _unit_tri_inverse(low, eye)
        u = _dot((tinv * b_row).astype(BF16), vc.astype(BF16))
        w = _dot((tinv * (b_row * jnp.exp(gc_row))).astype(BF16), kc_bf)
        qg = (qc * jnp.exp(gc_col)).astype(BF16)
        kd_t = jnp.transpose(kc * jnp.exp(g_tot - gc_col)).astype(BF16)
        idx = direction * hg + h
        state = st_s[idx]
        yield
        ws_qs = _dot(jnp.concatenate([w.astype(BF16), qg], axis=0), state.astype(BF16))
        yield
        v_new = (u - ws_qs[0:CHUNK]).astype(BF16)
        od = _dot(jnp.concatenate([qk.astype(BF16), kd_t], axis=0), v_new)
        yield
        new_state = state * jnp.exp(g_tot) + od[CHUNK:CHUNK + DN_DK]
        o = ws_qs[CHUNK:2 * CHUNK] + od[0:CHUNK]
        return idx, new_state, (of_s if direction == 0 else ob_s), r0, sl, o

    def body(n, carry):
        chains = []
        for h in range(hg):
            chains.append(chain_step(n, h, 0))
            chains.append(chain_step(n_chunks - 1 - n, h, 1))
        results = [None] * len(chains)
        while any(r is None for r in results):
            for ci, gen in enumerate(chains):
                if results[ci] is None:
                    try:
                        next(gen)
                    except StopIteration as done:
                        results[ci] = done.value
        for idx, new_state, dst, r0, sl, o in results:
            st_s[idx] = new_state
            dst[pl.ds(r0, CHUNK), sl] = o
        return carry

    lax.fori_loop(0, n_chunks, body, 0)
    if st_ref is not None:
        for d in range(2):
            for h in range(hg):
                st_ref[d, h] = st_s[d * hg + h]

    for h in range(hg):
        sl = slice(h * DN_DK, (h + 1) * DN_DK)
        o = of_s[:, sl] + ob_s[:, sl]
        og_ref[:, sl] = (_rms(o, ng_ref[...]) * _silu(z_ref[:, sl])).astype(BF16)


def _delta_core(main, gb, gt, conv_w, norm_gain, s0, layer_j, seq, hg):
    rows = main.shape[0]
    n_seq = rows // seq
    n_chunks = seq // CHUNK
    has_s0 = s0 is not None
    n_groups = DN_HEADS // hg
    width = hg * DN_DK

    def col_spec(group):
        return pl.BlockSpec((seq, width), lambda b, g: (b, group * n_groups + g))

    def conv_spec(group):
        return pl.BlockSpec((CONV_K, width), lambda b, g: (0, group * n_groups + g))

    in_specs = [
        col_spec(0), col_spec(1), col_spec(2), col_spec(3),
        pl.BlockSpec((seq, LANES), lambda b, g: (b, 0)),
        pl.BlockSpec((n_chunks, 6 * DN_HEADS, CHUNK), lambda b, g: (b, 0, 0)),
        conv_spec(0), conv_spec(1), conv_spec(2),
        pl.BlockSpec((1, DN_DK), lambda b, g: (0, 0)),
    ]
    args = [main, main, main, main, gb, gt, conv_w, conv_w, conv_w, norm_gain.reshape(1, DN_DK)]
    out_specs = [pl.BlockSpec((seq, width), lambda b, g: (b, g))]
    out_shape = [jax.ShapeDtypeStruct((rows, DN_KEY), BF16)]
    if has_s0:
        in_specs.append(pl.BlockSpec((None, None, 2, hg, DN_DK, DN_DK), lambda b, g: (b, layer_j, 0, g, 0, 0)))
        args.append(s0)
    else:
        out_specs.append(pl.BlockSpec((None, 2, hg, DN_DK, DN_DK), lambda b, g: (b, 0, g, 0, 0)))
        out_shape.append(jax.ShapeDtypeStruct((n_seq, 2, DN_HEADS, DN_DK, DN_DK), F32))
    scratch = [pltpu.VMEM((seq, width), F32) for _ in range(5)]
    scratch.append(pltpu.VMEM((seq + 16, width), F32))
    scratch.append(pltpu.VMEM((2 * hg, DN_DK, DN_DK), F32))
    return pl.pallas_call(
        functools.partial(_delta_kernel, seq=seq, hg=hg, has_s0=has_s0),
        grid=(n_seq, n_groups),
        in_specs=in_specs,
        out_specs=out_specs,
        out_shape=out_shape,
        scratch_shapes=scratch,
        compiler_params=_params(2),
        name="delta_core",
    )(*args)


def _post_kernel(*refs, final, ff_chunk):
    x_ref, o_ref, mods_ref, g2_ref, wout_ref, w1_ref, w2_ref = refs[:7]
    fin_ref = refs[7] if final else None
    out_ref = refs[-1]
    x1 = x_ref[...] + mods_ref[2:3, :] * _dot(o_ref[...], wout_ref[...])
    h = (_rms(x1, g2_ref[...]) * (1.0 + mods_ref[4:5, :]) + mods_ref[3:4, :]).astype(BF16)
    acc = jnp.zeros(x1.shape, F32)
    for c in range(D_FF // ff_chunk):
        a = _dot(h, w1_ref[:, c * ff_chunk:(c + 1) * ff_chunk])
        a = jnp.square(jnp.maximum(a, 0.0)).astype(BF16)
        acc = acc + _dot(a, w2_ref[c * ff_chunk:(c + 1) * ff_chunk, :])
    x2 = x1 + mods_ref[5:6, :] * acc
    if final:
        x2 = _rms(x2, fin_ref[...])
    out_ref[...] = x2


def _post(x, o, mods_l, cond_of_tile, gain2, wout_bf, w1_bf, w2_bf, final_gain, tm):
    rows = x.shape[0]
    final = final_gain is not None
    in_specs = [
        pl.BlockSpec((tm, D_MODEL), lambda i: (i, 0)),
        pl.BlockSpec((tm, D_MODEL), lambda i: (i, 0)),
        _mods_spec(cond_of_tile),
        _const_spec((1, D_MODEL)),
        _const_spec((D_MODEL, D_MODEL)),
        _const_spec((D_MODEL, D_FF)),
        _const_spec((D_FF, D_MODEL)),
    ]
    args = [x, o, mods_l, gain2.reshape(1, D_MODEL), wout_bf, w1_bf, w2_bf]
    if final:
        in_specs.append(_const_spec((1, D_MODEL)))
        args.append(final_gain.reshape(1, D_MODEL))
    return pl.pallas_call(
        functools.partial(_post_kernel, final=final, ff_chunk=1024),
        grid=(rows // tm,),
        in_specs=in_specs,
        out_specs=pl.BlockSpec((tm, D_MODEL), lambda i: (i, 0)),
        out_shape=jax.ShapeDtypeStruct((rows, D_MODEL), F32),
        compiler_params=_params(1),
        name="post_mlp",
    )(*args)


def _rope_tables(n_tokens):
    rows = n_tokens // GRID_W
    row = jnp.repeat(jnp.arange(rows), GRID_W).astype(F32)
    col = jnp.tile(jnp.arange(GRID_W), rows).astype(F32)
    inv = ROPE_THETA ** (-jnp.arange(ROPE_FREQS, dtype=F32) / ROPE_FREQS)
    ang_r = row[:, None] * inv
    ang_c = col[:, None] * inv
    cos = jnp.concatenate([jnp.cos(ang_r), jnp.cos(ang_r), jnp.cos(ang_c), jnp.cos(ang_c)], axis=1)
    sin = jnp.concatenate([-jnp.sin(ang_r), jnp.sin(ang_r), -jnp.sin(ang_c), jnp.sin(ang_c)], axis=1)
    cos = jnp.concatenate([jnp.ones_like(cos), cos], axis=0)
    sin = jnp.concatenate([jnp.zeros_like(sin), sin], axis=0)
    return cos, sin


def kernel(x_prompt, x_sample, cache_k, cache_v, state_delta, c, c_ctx, w_mod, b_mod, norm1, norm2,
           attn_w_in, attn_q_gain, attn_k_gain, attn_w_out, dn_w_in, dn_conv, dn_a_log, dn_dt_bias,
           dn_norm, dn_w_out, mlp_w1, mlp_w2, final_norm):
    n_ctx, seq_ctx, _ = x_prompt.shape
    n_lat, seq_lat, _ = x_sample.shape
    past = cache_k.shape[2]
    tm = 256

    cond = jnp.concatenate([c_ctx[None, :], c, jnp.zeros((N_COND - 1 - n_lat, D_MODEL), F32)], axis=0)
    mods = _modulation(cond, w_mod, b_mod).reshape(DEPTH, N_COND, N_MOD, D_MODEL)

    rope_cos, rope_sin = _rope_tables(seq_lat)
    lat_tiles = seq_lat // tm
    cache_k4 = cache_k.reshape(n_lat, cache_k.shape[1], past, KV_WIDTH)
    cache_v4 = cache_v.reshape(n_lat, cache_v.shape[1], past, KV_WIDTH)

    streams = {
        "ctx": dict(x=x_prompt.reshape(n_ctx * seq_ctx, D_MODEL), seq=seq_ctx,
                    cond=lambda i: 0, rope=lambda i: 0),
        "lat": dict(x=x_sample.reshape(n_lat * seq_lat, D_MODEL), seq=seq_lat,
                    cond=lambda i: 1 + i // lat_tiles, rope=lambda i: lat_tiles + i % lat_tiles),
    }
    keys_out, vals_out, states_out = [], [], []

    for l in range(DEPTH):
        j = l // 2
        mods_l = mods[l]
        w1_bf = mlp_w1[l].astype(BF16)
        w2_bf = mlp_w2[l].astype(BF16)
        final_gain = final_norm if l == DEPTH - 1 else None
        if l % 2 == 0:
            w_in_bf = attn_w_in[j].astype(BF16)
            w_out_bf = attn_w_out[j].astype(BF16)
        else:
            w_main_bf = dn_w_in[j][:, :DN_MAIN].astype(BF16)
            w_ab_bf = jnp.pad(dn_w_in[j][:, DN_MAIN:], ((0, 0), (0, LANES - 4 * DN_HEADS))).astype(BF16)
            w_out_bf = dn_w_out[j].astype(BF16)
            alog_row = jnp.pad(dn_a_log[j].reshape(1, 2 * DN_HEADS), ((0, 0), (0, LANES - 2 * DN_HEADS)))
            dtb_row = jnp.pad(dn_dt_bias[j].reshape(1, 2 * DN_HEADS), ((0, 0), (0, LANES - 2 * DN_HEADS)))
        for name, st in streams.items():
            x = st["x"]
            seq = st["seq"]
            rows = x.shape[0]
            if l % 2 == 0:
                q, k, v = _attn_proj(x, mods_l, st["cond"], norm1[l], w_in_bf, attn_q_gain[j], attn_k_gain[j],
                                     rope_cos, rope_sin, st["rope"], tm)
                if name == "ctx":
                    keys_out.append(k.reshape(n_ctx, seq, KV_HEADS, HEAD_DIM))
                    vals_out.append(v.reshape(n_ctx, seq, KV_HEADS, HEAD_DIM))
                    o = _context_attention(q, k, v, seq)
                else:
                    o = _latent_attention(q, k, v, cache_k4, cache_v4, j, seq, tm)
            else:
                main, gb = _dn_proj(x, mods_l, st["cond"], norm1[l], w_main_bf, w_ab_bf, alog_row, dtb_row, tm)
                gt = gb[:, :6 * DN_HEADS].reshape(rows // CHUNK, CHUNK, 6 * DN_HEADS).transpose(0, 2, 1)
                if name == "ctx":
                    o, states = _delta_core(main, gb, gt, dn_conv[j], dn_norm[j], None, j, seq, DN_HEADS)
                    states_out.append(states)
                else:
                    (o,) = _delta_core(main, gb, gt, dn_conv[j], dn_norm[j], state_delta, j, seq, DN_HEADS // 2)
            st["x"] = _post(x, o, mods_l, st["cond"], norm2[l], w_out_bf, w1_bf, w2_bf, final_gain, tm)

    y_prompt = streams["ctx"]["x"].reshape(n_ctx, seq_ctx, D_MODEL)
    y_sample = streams["lat"]["x"].reshape(n_lat, seq_lat, D_MODEL)
    new_cache_k = jnp.stack(keys_out, axis=1)
    new_cache_v = jnp.stack(vals_out, axis=1)
    new_state_delta = jnp.stack(states_out, axis=1)
    return (y_prompt, y_sample, new_cache_k, new_cache_v, new_state_delta)
```

```python
import functools

import jax
import jax.numpy as jnp
from jax import lax
from jax.experimental import pallas as pl
from jax.experimental.pallas import tpu as pltpu

F32 = jnp.float32
BF16 = jnp.bfloat16

D_MODEL = 1024
DEPTH = 4
GRID_W = 64
HEAD_DIM = 128
N_HEADS = 8
KV_HEADS = 2
GROUP = N_HEADS // KV_HEADS
Q_WIDTH = N_HEADS * HEAD_DIM
KV_WIDTH = KV_HEADS * HEAD_DIM
ROPE_THETA = 10000.0
ROPE_FREQS = HEAD_DIM // 4
DN_HEADS = 8
DN_DK = 128
DN_KEY = DN_HEADS * DN_DK
DN_MAIN = 4 * DN_KEY
CONV_K = 5
CHUNK = 64
MAX_UNROLLED_CHUNKS = 4
CHUNKS_PER_STEP = 2
D_FF = 4 * D_MODEL
N_MOD = 6
EPS = 1e-6
N_COND = 8
LANES = 128
VMEM_LIMIT = 52 * 1024 * 1024


def _params(n_grid):
    return pltpu.CompilerParams(dimension_semantics=("arbitrary",) * n_grid,
                                vmem_limit_bytes=VMEM_LIMIT)


def _const_spec(shape):
    nd = len(shape)
    return pl.BlockSpec(shape, lambda *_: (0,) * nd)


def _dot(a, b):
    return jnp.dot(a, b, preferred_element_type=F32)


def _dot_nt(a, b):
    return lax.dot_general(a, b, (((1,), (1,)), ((), ())), preferred_element_type=F32)


def _silu(x):
    return x * jax.nn.sigmoid(x)


def _rms(x, gain):
    return x * lax.rsqrt(jnp.mean(x * x, axis=-1, keepdims=True) + EPS) * gain


def _mod_kernel(cond_ref, w_ref, b_ref, o_ref):
    s = _silu(cond_ref[...]).astype(BF16)
    o_ref[...] = _dot(s, w_ref[...].astype(BF16)) + b_ref[...]


def _modulation(cond, w_mod, b_mod):
    tn = 1536
    n_out = N_MOD * D_MODEL
    return pl.pallas_call(
        _mod_kernel,
        grid=(DEPTH, n_out // tn),
        in_specs=[
            _const_spec((N_COND, D_MODEL)),
            pl.BlockSpec((None, D_MODEL, tn), lambda l, n: (l, 0, n)),
            pl.BlockSpec((None, 1, tn), lambda l, n: (l, 0, n)),
        ],
        out_specs=pl.BlockSpec((None, N_COND, tn), lambda l, n: (l, 0, n)),
        out_shape=jax.ShapeDtypeStruct((DEPTH, N_COND, n_out), F32),
        compiler_params=_params(2),
        name="modulation",
    )(cond, w_mod, b_mod.reshape(DEPTH, 1, n_out))


def _mods_spec(cond_of_tile):
    return pl.BlockSpec((None, N_MOD, D_MODEL), lambda i: (cond_of_tile(i), 0, 0))


def _attn_proj_kernel(x_ref, mods_ref, gain_ref, w_ref, qg_ref, kg_ref, cos_ref, sin_ref, *rest):
    q_out, k_out, v_out = rest[-3:]
    x = x_ref[...]
    h = _rms(x, gain_ref[...]) * (1.0 + mods_ref[1:2, :]) + mods_ref[0:1, :]
    proj = _dot(h.astype(BF16), w_ref[...])
    cos = cos_ref[...]
    sin = sin_ref[...]
    lane = lax.broadcasted_iota(jnp.int32, cos.shape, 1)
    low_half = (lane % (2 * ROPE_FREQS)) < ROPE_FREQS

    def prep(xh, gain):
        y = _rms(xh, gain)
        partner = jnp.where(low_half, pltpu.roll(y, LANES - ROPE_FREQS, 1), pltpu.roll(y, ROPE_FREQS, 1))
        return y * cos + partner * sin

    for hh in range(N_HEADS):
        sl = slice(hh * HEAD_DIM, (hh + 1) * HEAD_DIM)
        q_out[:, sl] = (prep(proj[:, sl], qg_ref[...]) * (HEAD_DIM ** -0.5)).astype(BF16)
    if len(k_out.shape) == 3:
        for slot in range(1, k_out.shape[0]):
            k_out[slot] = jnp.zeros(k_out.shape[1:], F32)
            v_out[slot] = jnp.zeros(v_out.shape[1:], F32)
        k_out, v_out = k_out.at[0], v_out.at[0]
    for kv in range(KV_HEADS):
        sl = slice(kv * HEAD_DIM, (kv + 1) * HEAD_DIM)
        k_out[:, sl] = prep(proj[:, Q_WIDTH + kv * HEAD_DIM:Q_WIDTH + (kv + 1) * HEAD_DIM], kg_ref[...])
    v_out[...] = proj[:, Q_WIDTH + KV_WIDTH:]


def _attn_proj(x, mods_l, cond_of_tile, gain, w_bf, q_gain, k_gain, rope_cos, rope_sin, rope_of_tile, tm,
               layered=None):
    rows = x.shape[0]
    in_specs = [
        pl.BlockSpec((tm, D_MODEL), lambda i: (i, 0)),
        _mods_spec(cond_of_tile),
        _const_spec((1, D_MODEL)),
        _const_spec((D_MODEL, Q_WIDTH + 2 * KV_WIDTH)),
        _const_spec((1, HEAD_DIM)),
        _const_spec((1, HEAD_DIM)),
        pl.BlockSpec((tm, HEAD_DIM), lambda i: (rope_of_tile(i), 0)),
        pl.BlockSpec((tm, HEAD_DIM), lambda i: (rope_of_tile(i), 0)),
    ]
    args = [x, mods_l, gain.reshape(1, D_MODEL), w_bf, q_gain.reshape(1, HEAD_DIM), k_gain.reshape(1, HEAD_DIM),
            rope_cos, rope_sin]
    aliases = {}
    if layered is None:
        kv_spec = pl.BlockSpec((tm, KV_WIDTH), lambda i: (i, 0))
        kv_shape = jax.ShapeDtypeStruct((rows, KV_WIDTH), F32)
    else:
        layer_j, n_layers, prev_k, prev_v = layered
        if prev_k is None:
            assert layer_j == 0
            kv_spec = pl.BlockSpec((None, n_layers, tm, KV_WIDTH), lambda i: (i, 0, 0, 0))
        else:
            kv_spec = pl.BlockSpec((None, None, tm, KV_WIDTH), lambda i: (i, layer_j, 0, 0))
        kv_shape = jax.ShapeDtypeStruct((rows // tm, n_layers, tm, KV_WIDTH), F32)
        if prev_k is not None:
            aliases = {len(args): 1, len(args) + 1: 2}
            in_specs += [pl.BlockSpec(memory_space=pl.ANY), pl.BlockSpec(memory_space=pl.ANY)]
            args += [prev_k, prev_v]
    return pl.pallas_call(
        _attn_proj_kernel,
        grid=(rows // tm,),
        in_specs=in_specs,
        out_specs=[pl.BlockSpec((tm, Q_WIDTH), lambda i: (i, 0)), kv_spec, kv_spec],
        out_shape=[jax.ShapeDtypeStruct((rows, Q_WIDTH), BF16), kv_shape, kv_shape],
        input_output_aliases=aliases,
        compiler_params=_params(1),
        name="attn_proj",
    )(*args)


def _attn_kernel(*refs, n_pieces):
    q_ref = refs[0]
    kv_refs = refs[1:1 + 2 * n_pieces]
    o_ref = refs[1 + 2 * n_pieces]
    q = q_ref[...]
    for kv in range(KV_HEADS):
        sl = slice(kv * HEAD_DIM, (kv + 1) * HEAD_DIM)
        ks = [kv_refs[2 * p][:, sl].astype(BF16) for p in range(n_pieces)]
        vs = [kv_refs[2 * p + 1][:, sl].astype(BF16) for p in range(n_pieces)]
        for g in range(GROUP):
            hh = kv * GROUP + g
            qh = q[:, hh * HEAD_DIM:(hh + 1) * HEAD_DIM]
            ss = [_dot_nt(qh, k) for k in ks]
            m = functools.reduce(jnp.maximum, [jnp.max(s, axis=-1, keepdims=True) for s in ss])
            ps = [jnp.exp(s - m) for s in ss]
            denom = functools.reduce(jnp.add, [jnp.sum(p, axis=-1, keepdims=True) for p in ps])
            o = functools.reduce(jnp.add, [_dot(p.astype(BF16), v) for p, v in zip(ps, vs)])
            o_ref[:, hh * HEAD_DIM:(hh + 1) * HEAD_DIM] = (o / denom).astype(BF16)


def _context_attention(q, k4, v4, layer_j, seq):
    rows = q.shape[0]
    kv_spec = pl.BlockSpec((None, None, seq, KV_WIDTH), lambda b: (b, layer_j, 0, 0))
    return pl.pallas_call(
        functools.partial(_attn_kernel, n_pieces=1),
        grid=(rows // seq,),
        in_specs=[pl.BlockSpec((seq, Q_WIDTH), lambda b: (b, 0)), kv_spec, kv_spec],
        out_specs=pl.BlockSpec((seq, Q_WIDTH), lambda b: (b, 0)),
        out_shape=jax.ShapeDtypeStruct((rows, Q_WIDTH), BF16),
        compiler_params=_params(1),
        name="context_attention",
    )(q, k4, v4)


def _latent_attention(q, k, v, cache_k4, cache_v4, layer_j, seq, tq):
    rows = q.shape[0]
    past = cache_k4.shape[2]
    nq = seq // tq
    cache_spec = pl.BlockSpec((None, None, past, KV_WIDTH), lambda b, i: (b, layer_j, 0, 0))
    return pl.pallas_call(
        functools.partial(_attn_kernel, n_pieces=2),
        grid=(rows // seq, nq),
        in_specs=[
            pl.BlockSpec((tq, Q_WIDTH), lambda b, i: (b * nq + i, 0)),
            cache_spec,
            cache_spec,
            pl.BlockSpec((seq, KV_WIDTH), lambda b, i: (b, 0)),
            pl.BlockSpec((seq, KV_WIDTH), lambda b, i: (b, 0)),
        ],
        out_specs=pl.BlockSpec((tq, Q_WIDTH), lambda b, i: (b * nq + i, 0)),
        out_shape=jax.ShapeDtypeStruct((rows, Q_WIDTH), BF16),
        compiler_params=_params(2),
        name="latent_attention",
    )(q, cache_k4, cache_v4, k, v)


def _dn_proj_kernel(x_ref, mods_ref, gain_ref, w_ref, wab_ref, alog_ref, dtb_ref, main_out, gb_out, gt_out):
    x = x_ref[...]
    h = (_rms(x, gain_ref[...]) * (1.0 + mods_ref[1:2, :]) + mods_ref[0:1, :]).astype(BF16)
    main_out[...] = _dot(h, w_ref[...])
    ab = _dot(h, wab_ref[...])
    lane = lax.broadcasted_iota(jnp.int32, ab.shape, 1)
    g = -jnp.exp(alog_ref[...]) * jax.nn.softplus(ab + dtb_ref[...])
    tm = ab.shape[0]
    r = lax.broadcasted_iota(jnp.int32, (tm, tm), 0)
    c = lax.broadcasted_iota(jnp.int32, (tm, tm), 1)
    same_chunk = (r // CHUNK) == (c // CHUNK)
    m_fwd = jnp.where(same_chunk, jnp.where(c <= r, 1.0, 0.0), 0.0).astype(BF16)
    m_bwd = jnp.where(same_chunk, jnp.where(c >= r, 1.0, 0.0), 0.0).astype(BF16)
    g1 = g.astype(BF16)
    rem = g - g1.astype(F32)
    g2 = rem.astype(BF16)
    g3 = (rem - g2.astype(F32)).astype(BF16)
    cum_f = _dot(m_fwd, g1) + (_dot(m_fwd, g2) + _dot(m_fwd, g3))
    cum_b = _dot(m_bwd, g1) + (_dot(m_bwd, g2) + _dot(m_bwd, g3))
    total = cum_f + cum_b - g
    out = jnp.where(lane < DN_HEADS, cum_f, jnp.where(lane < 2 * DN_HEADS, cum_b, jax.nn.sigmoid(ab)))
    table = jnp.where(lane < 4 * DN_HEADS, out, pltpu.roll(total, 4 * DN_HEADS, 1))
    gb_out[...] = table
    for ch in range(tm // CHUNK):
        gt_out[ch] = jnp.transpose(table[ch * CHUNK:(ch + 1) * CHUNK, :])


def _dn_proj(x, mods_l, cond_of_tile, gain, w_main_bf, w_ab_bf, alog_row, dtb_row, tm):
    rows = x.shape[0]
    return pl.pallas_call(
        _dn_proj_kernel,
        grid=(rows // tm,),
        in_specs=[
            pl.BlockSpec((tm, D_MODEL), lambda i: (i, 0)),
            _mods_spec(cond_of_tile),
            _const_spec((1, D_MODEL)),
            _const_spec((D_MODEL, DN_MAIN)),
            _const_spec((D_MODEL, LANES)),
            _const_spec((1, LANES)),
            _const_spec((1, LANES)),
        ],
        out_specs=[
            pl.BlockSpec((tm, DN_MAIN), lambda i: (i, 0)),
            pl.BlockSpec((tm, LANES), lambda i: (i, 0)),
            pl.BlockSpec((tm // CHUNK, LANES, CHUNK), lambda i: (i, 0, 0)),
        ],
        out_shape=[
            jax.ShapeDtypeStruct((rows, DN_MAIN), F32),
            jax.ShapeDtypeStruct((rows, LANES), F32),
            jax.ShapeDtypeStruct((rows // CHUNK, LANES, CHUNK), F32),
        ],
        compiler_params=_params(1),
        name="dn_proj",
    )(x, mods_l, gain.reshape(1, D_MODEL), w_main_bf, w_ab_bf, alog_row, dtb_row)


def _split_bf16(a):
    hi = a.astype(BF16)
    lo = (a - hi.astype(F32)).astype(BF16)
    return hi, lo


def _unit_tri_inverse_pair(low, eye, block_diag_mask):
    c = CHUNK

    def block_diag(x_bf):
        return jnp.where(block_diag_mask, jnp.concatenate([x_bf, x_bf], axis=0), jnp.zeros((), BF16))

    p = eye - low
    m = low
    steps = CHUNK.bit_length() - 2
    for it in range(steps + 1):
        need_square = it < steps
        need_pm = it > 0
        mh, ml = _split_bf16(m)
        lhs_a, lhs_b = [], []
        if need_square:
            lhs_a += [mh, ml]
            lhs_b += [mh]
        if need_pm:
            ph, pl_ = _split_bf16(p)
            lhs_a += [ph, pl_]
            lhs_b += [ph]
        a = _dot(jnp.concatenate(lhs_a, axis=0), block_diag(mh))
        b = _dot(jnp.concatenate(lhs_b, axis=0), block_diag(ml))
        off_a = off_b = 0
        if need_square:
            m = a[0:c] + a[c:2 * c] + b[0:c]
            off_a, off_b = 2 * c, c
        if need_pm:
            p = p + (a[off_a:off_a + c] + a[off_a + c:off_a + 2 * c] + b[off_b:off_b + c])
        yield
    return p


def _delta_kernel(*refs, seq, hg, has_s0, has_prev):
    (q_ref, k_ref, v_ref, z_ref, gb_ref, gt_ref, cq_ref, ck_ref, cv_ref, ng_ref) = refs[:10]
    pos = 10
    s0_ref = None
    if has_s0:
        s0_ref = refs[pos]
        pos += 1
    if has_prev:
        pos += 1
    og_ref = refs[pos]
    pos += 1
    st_ref = None
    if not has_s0:
        st_ref = refs[pos]
        pos += 1
    q_s, k_s, v_s, of_s, ob_s, pad_q, pad_k, pad_v, st_s = refs[pos:]

    head0 = pl.program_id(1) * hg
    n_chunks = seq // CHUNK
    halo = 8
    width = hg * DN_DK
    unrolled = n_chunks <= MAX_UNROLLED_CHUNKS

    for pad, x_ref in ((pad_q, q_ref), (pad_k, k_ref), (pad_v, v_ref)):
        pad[0:halo, :] = jnp.zeros((halo, width), F32)
        pad[halo + seq:2 * halo + seq, :] = jnp.zeros((halo, width), F32)
        pad[halo:halo + seq, :] = x_ref[...]

    def prep_chunk(c):
        r = c * CHUNK
        base = halo - CONV_K // 2 + r

        def conv_silu(pad, cw_ref):
            acc = pad[base:base + CHUNK, :] * cw_ref[0:1, :]
            for j in range(1, CONV_K):
                acc = acc + pad[base + j:base + j + CHUNK, :] * cw_ref[j:j + 1, :]
            return _silu(acc)

        def l2n_heads(x, scale, dst):
            for h in range(hg):
                sl = slice(h * DN_DK, (h + 1) * DN_DK)
                xh = x[:, sl]
                dst[r:r + CHUNK, sl] = xh * (lax.rsqrt(jnp.sum(xh * xh, axis=-1, keepdims=True) + EPS) * scale)

        l2n_heads(conv_silu(pad_q, cq_ref), DN_DK ** -0.5, q_s)
        l2n_heads(conv_silu(pad_k, ck_ref), 1.0, k_s)
        v_s[r:r + CHUNK, :] = conv_silu(pad_v, cv_ref)

    order = []
    for c in range(n_chunks):
        for cc in (c, n_chunks - 1 - c):
            if cc not in order:
                order.append(cc)
    for c in order:
        prep_chunk(c)

    for d in range(2):
        for h in range(hg):
            if has_s0:
                st_s[d * hg + h] = s0_ref[d, h]
            else:
                st_s[d * hg + h] = jnp.zeros((DN_DK, DN_DK), F32)

    pair_w = 2 * CHUNK
    row_p = lax.broadcasted_iota(jnp.int32, (CHUNK, pair_w), 0)
    lane_p = lax.broadcasted_iota(jnp.int32, (CHUNK, pair_w), 1)
    col_p = lane_p & (CHUNK - 1)
    in_b = lane_p >= CHUNK
    eye = (row_p == col_p).astype(F32)
    bd_r = lax.broadcasted_iota(jnp.int32, (pair_w, pair_w), 0)
    bd_c = lax.broadcasted_iota(jnp.int32, (pair_w, pair_w), 1)
    block_diag_mask = (bd_r >= CHUNK) == (bd_c >= CHUNK)
    lane = lax.broadcasted_iota(jnp.int32, (CHUNK, LANES), 1)
    zeros_k = jnp.zeros((CHUNK, DN_DK), BF16)

    def pair_step(n, ha, direction):
        hb = ha + 1
        r0 = n * CHUNK if unrolled else pl.multiple_of(n * CHUNK, CHUNK)
        sls = [slice(h * DN_DK, (h + 1) * DN_DK) for h in (ha, hb)]
        qs = [q_s[pl.ds(r0, CHUNK), sl] for sl in sls]
        ks = [k_s[pl.ds(r0, CHUNK), sl] for sl in sls]
        vs = [v_s[pl.ds(r0, CHUNK), sl].astype(BF16) for sl in sls]
        ks_bf = [k.astype(BF16) for k in ks]
        gbc = gb_ref[pl.ds(r0, CHUNK), :]
        col_a = direction * DN_HEADS + head0 + ha

        def col_vec(c):
            return jnp.sum(jnp.where(lane == c, gbc, 0.0), axis=1, keepdims=True)

        def row_vec(r):
            both = gt_ref[n, pl.ds(r, 2), :]
            return jnp.concatenate([both[0:1], both[1:2]], axis=1)

        gcs = [col_vec(col_a), col_vec(col_a + 1)]
        gc_col = jnp.where(in_b, gcs[1], gcs[0])
        b_col = jnp.where(in_b, col_vec(col_a + 1 + 2 * DN_HEADS), col_vec(col_a + 2 * DN_HEADS))
        gc_row = row_vec(col_a)
        b_row = row_vec(col_a + 2 * DN_HEADS)
        tot = gt_ref[n, pl.ds(col_a + 4 * DN_HEADS, 2), :]
        g_tots = [tot[0:1, 0:1], tot[1:2, 0:1]]
        if direction == 0:
            incl = col_p <= row_p
            strict = col_p < row_p
        else:
            incl = col_p >= row_p
            strict = col_p > row_p
        decay = jnp.exp(jnp.where(incl, gc_col - gc_row, 0.0))
        kq = (_dot_nt(jnp.concatenate([ks_bf[0], qs[0].astype(BF16)], axis=0),
                      jnp.concatenate([ks_bf[0], zeros_k], axis=0))
              + _dot_nt(jnp.concatenate([ks_bf[1], qs[1].astype(BF16)], axis=0),
                        jnp.concatenate([zeros_k, ks_bf[1]], axis=0)))
        yield
        low = jnp.where(strict, kq[0:CHUNK] * (decay * b_col), 0.0)
        qk = jnp.where(incl, kq[CHUNK:2 * CHUNK] * decay, 0.0).astype(BF16)
        tinv = yield from _unit_tri_inverse_pair(low, eye, block_diag_mask)
        t_u = (tinv * b_row).astype(BF16)
        t_w = (tinv * (b_row * jnp.exp(gc_row))).astype(BF16)
        pick = [lambda x: jnp.concatenate([x, zeros_k], axis=0), lambda x: jnp.concatenate([zeros_k, x], axis=0)]
        us = [_dot(t_u, pick[i](vs[i])) for i in range(2)]
        ws = [_dot(t_w, pick[i](ks_bf[i])) for i in range(2)]
        qgs = [(qs[i] * jnp.exp(gcs[i])).astype(BF16) for i in range(2)]
        kd = jnp.concatenate([ks[i] * jnp.exp(g_tots[i] - gcs[i]) for i in range(2)], axis=0)
        kd_t = jnp.transpose(kd).astype(BF16)
        yield
        return dict(us=us, ws=[w.astype(BF16) for w in ws], qgs=qgs, lhs_od=jnp.concatenate([qk, kd_t], axis=0),
                    decay_tot=[jnp.exp(g) for g in g_tots], pick=pick, r0=r0, sls=sls,
                    dst=of_s if direction == 0 else ob_s)

    def pair_scan(a, states):
        ws_qs = [_dot(jnp.concatenate([a["ws"][i], a["qgs"][i]], axis=0), states[i].astype(BF16))
                 for i in range(2)]
        yield
        v_new = [(a["us"][i] - ws_qs[i][0:CHUNK]).astype(BF16) for i in range(2)]
        ods = [_dot(a["lhs_od"], a["pick"][i](v_new[i])) for i in range(2)]
        yield
        new_states = [states[i] * a["decay_tot"][i] + ods[i][CHUNK:CHUNK + DN_DK] for i in range(2)]
        outs = [ws_qs[i][CHUNK:2 * CHUNK] + ods[i][0:CHUNK] for i in range(2)]
        return new_states, outs

    def round_robin(gens):
        results = [None] * len(gens)
        while any(r is None for r in results):
            for gi, gen in enumerate(gens):
                if results[gi] is None:
                    try:
                        next(gen)
                    except StopIteration as done:
                        results[gi] = done.value
        return results

    lanes_of = [(ha, d) for ha in range(0, hg, 2) for d in range(2)]

    def body(it, carry):
        def chunk_of(s, d):
            c = it * sub + s
            return c if d == 0 else n_chunks - 1 - c

        preps = round_robin([pair_step(chunk_of(s, d), ha, d) for s in range(sub) for ha, d in lanes_of])
        states = [[st_s[d * hg + ha], st_s[d * hg + ha + 1]] for ha, d in lanes_of]
        stores = []
        for s in range(sub):
            mine = preps[s * len(lanes_of):(s + 1) * len(lanes_of)]
            scanned = round_robin([pair_scan(a, st) for a, st in zip(mine, states)])
            states = [new for new, _ in scanned]
            stores += [(a, outs) for a, (_, outs) in zip(mine, scanned)]
        for (ha, d), st in zip(lanes_of, states):
            st_s[d * hg + ha] = st[0]
            st_s[d * hg + ha + 1] = st[1]
        for a, outs in stores:
            for i in range(2):
                a["dst"][pl.ds(a["r0"], CHUNK), a["sls"][i]] = outs[i]
        return carry

    if unrolled:
        sub = 1
        for n in range(n_chunks):
            body(n, 0)
    else:
        sub = CHUNKS_PER_STEP
        lax.fori_loop(0, n_chunks // sub, body, 0)
    if st_ref is not None:
        if len(st_ref.shape) == 5:
            for slot in range(1, st_ref.shape[0]):
                st_ref[slot] = jnp.zeros(st_ref.shape[1:], F32)
            st_ref = st_ref.at[0]
        for d in range(2):
            for h in range(hg):
                st_ref[d, h] = st_s[d * hg + h]

    for h in range(hg):
        sl = slice(h * DN_DK, (h + 1) * DN_DK)
        o = of_s[:, sl] + ob_s[:, sl]
        og_ref[:, sl] = (_rms(o, ng_ref[...]) * _silu(z_ref[:, sl])).astype(BF16)


def _delta_core(main, gb, gt, conv_w, norm_gain, s0, layer_j, seq, hg, n_layers=None, prev_states=None):
    rows = main.shape[0]
    n_seq = rows // seq
    n_chunks = seq // CHUNK
    has_s0 = s0 is not None
    n_groups = DN_HEADS // hg
    width = hg * DN_DK
    aliases = {}

    def col_spec(group):
        return pl.BlockSpec((seq, width), lambda b, g: (b, group * n_groups + g))

    def conv_spec(group):
        return pl.BlockSpec((CONV_K, width), lambda b, g: (0, group * n_groups + g))

    in_specs = [
        col_spec(0), col_spec(1), col_spec(2), col_spec(3),
        pl.BlockSpec((seq, LANES), lambda b, g: (b, 0)),
        pl.BlockSpec((n_chunks, LANES, CHUNK), lambda b, g: (b, 0, 0)),
        conv_spec(0), conv_spec(1), conv_spec(2),
        pl.BlockSpec((1, DN_DK), lambda b, g: (0, 0)),
    ]
    args = [main, main, main, main, gb, gt, conv_w, conv_w, conv_w, norm_gain.reshape(1, DN_DK)]
    out_specs = [pl.BlockSpec((seq, width), lambda b, g: (b, g))]
    out_shape = [jax.ShapeDtypeStruct((rows, DN_KEY), BF16)]
    if has_s0:
        in_specs.append(pl.BlockSpec((None, None, 2, hg, DN_DK, DN_DK), lambda b, g: (b, layer_j, 0, g, 0, 0)))
        args.append(s0)
    else:
        if prev_states is None:
            assert layer_j == 0
            out_specs.append(pl.BlockSpec((None, n_layers, 2, hg, DN_DK, DN_DK), lambda b, g: (b, 0, 0, g, 0, 0)))
        else:
            out_specs.append(pl.BlockSpec((None, None, 2, hg, DN_DK, DN_DK), lambda b, g: (b, layer_j, 0, g, 0, 0)))
        out_shape.append(jax.ShapeDtypeStruct((n_seq, n_layers, 2, DN_HEADS, DN_DK, DN_DK), F32))
        if prev_states is not None:
            aliases = {len(args): 1}
            in_specs.append(pl.BlockSpec(memory_space=pl.ANY))
            args.append(prev_states)
    scratch = [pltpu.VMEM((seq, width), F32) for _ in range(5)]
    scratch += [pltpu.VMEM((seq + 16, width), F32) for _ in range(3)]
    scratch.append(pltpu.VMEM((2 * hg, DN_DK, DN_DK), F32))
    return pl.pallas_call(
        functools.partial(_delta_kernel, seq=seq, hg=hg, has_s0=has_s0, has_prev=bool(aliases)),
        grid=(n_seq, n_groups),
        in_specs=in_specs,
        out_specs=out_specs,
        out_shape=out_shape,
        scratch_shapes=scratch,
        input_output_aliases=aliases,
        compiler_params=_params(2),
        name="delta_core",
    )(*args)


def _post_kernel(*refs, final, ff_chunk):
    x_ref, o_ref, mods_ref, g2_ref, wout_ref, w1_ref, w2_ref = refs[:7]
    fin_ref = refs[7] if final else None
    out_ref = refs[-1]
    x1 = x_ref[...] + mods_ref[2:3, :] * _dot(o_ref[...], wout_ref[...])
    h = (_rms(x1, g2_ref[...]) * (1.0 + mods_ref[4:5, :]) + mods_ref[3:4, :]).astype(BF16)
    acc = jnp.zeros(x1.shape, F32)
    for c in range(D_FF // ff_chunk):
        a = _dot(h, w1_ref[:, c * ff_chunk:(c + 1) * ff_chunk])
        a = jnp.square(jnp.maximum(a, 0.0)).astype(BF16)
        acc = acc + _dot(a, w2_ref[c * ff_chunk:(c + 1) * ff_chunk, :])
    x2 = x1 + mods_ref[5:6, :] * acc
    if final:
        x2 = _rms(x2, fin_ref[...])
    out_ref[...] = x2


def _post(x, o, mods_l, cond_of_tile, gain2, wout_bf, w1_bf, w2_bf, final_gain, tm):
    rows = x.shape[0]
    final = final_gain is not None
    in_specs = [
        pl.BlockSpec((tm, D_MODEL), lambda i: (i, 0)),
        pl.BlockSpec((tm, D_MODEL), lambda i: (i, 0)),
        _mods_spec(cond_of_tile),
        _const_spec((1, D_MODEL)),
        _const_spec((D_MODEL, D_MODEL)),
        _const_spec((D_MODEL, D_FF)),
        _const_spec((D_FF, D_MODEL)),
    ]
    args = [x, o, mods_l, gain2.reshape(1, D_MODEL), wout_bf, w1_bf, w2_bf]
    if final:
        in_specs.append(_const_spec((1, D_MODEL)))
        args.append(final_gain.reshape(1, D_MODEL))
    return pl.pallas_call(
        functools.partial(_post_kernel, final=final, ff_chunk=1024),
        grid=(rows // tm,),
        in_specs=in_specs,
        out_specs=pl.BlockSpec((tm, D_MODEL), lambda i: (i, 0)),
        out_shape=jax.ShapeDtypeStruct((rows, D_MODEL), F32),
        compiler_params=_params(1),
        name="post_mlp",
    )(*args)


def _rope_tables(n_tokens):
    rows = n_tokens // GRID_W
    row = jnp.repeat(jnp.arange(rows), GRID_W).astype(F32)
    col = jnp.tile(jnp.arange(GRID_W), rows).astype(F32)
    inv = ROPE_THETA ** (-jnp.arange(ROPE_FREQS, dtype=F32) / ROPE_FREQS)
    ang_r = row[:, None] * inv
    ang_c = col[:, None] * inv
    cos = jnp.concatenate([jnp.cos(ang_r), jnp.cos(ang_r), jnp.cos(ang_c), jnp.cos(ang_c)], axis=1)
    sin = jnp.concatenate([-jnp.sin(ang_r), jnp.sin(ang_r), -jnp.sin(ang_c), jnp.sin(ang_c)], axis=1)
    cos = jnp.concatenate([jnp.ones_like(cos), cos], axis=0)
    sin = jnp.concatenate([jnp.zeros_like(sin), sin], axis=0)
    return cos, sin


def kernel(x_prompt, x_sample, cache_k, cache_v, state_delta, c, c_ctx, w_mod, b_mod, norm1, norm2,
           attn_w_in, attn_q_gain, attn_k_gain, attn_w_out, dn_w_in, dn_conv, dn_a_log, dn_dt_bias,
           dn_norm, dn_w_out, mlp_w1, mlp_w2, final_norm):
    n_ctx, seq_ctx, _ = x_prompt.shape
    n_lat, seq_lat, _ = x_sample.shape
    past = cache_k.shape[2]
    tm = 256

    cond = jnp.concatenate([c_ctx[None, :], c, jnp.zeros((N_COND - 1 - n_lat, D_MODEL), F32)], axis=0)
    mods = _modulation(cond, w_mod, b_mod).reshape(DEPTH, N_COND, N_MOD, D_MODEL)

    rope_cos, rope_sin = _rope_tables(seq_lat)
    lat_tiles = seq_lat // tm
    cache_k4 = cache_k.reshape(n_lat, cache_k.shape[1], past, KV_WIDTH)
    cache_v4 = cache_v.reshape(n_lat, cache_v.shape[1], past, KV_WIDTH)

    streams = {
        "ctx": dict(x=x_prompt.reshape(n_ctx * seq_ctx, D_MODEL), seq=seq_ctx,
                    cond=lambda i: 0, rope=lambda i: 0),
        "lat": dict(x=x_sample.reshape(n_lat * seq_lat, D_MODEL), seq=seq_lat,
                    cond=lambda i: 1 + i // lat_tiles, rope=lambda i: lat_tiles + i % lat_tiles),
    }
    n_attn = (DEPTH + 1) // 2
    n_delta = DEPTH // 2
    new_k = new_v = new_states = None
    assert tm == seq_ctx

    for l in range(DEPTH):
        j = l // 2
        mods_l = mods[l]
        w1_bf = mlp_w1[l].astype(BF16)
        w2_bf = mlp_w2[l].astype(BF16)
        final_gain = final_norm if l == DEPTH - 1 else None
        if l % 2 == 0:
            w_in_bf = attn_w_in[j].astype(BF16)
            w_out_bf = attn_w_out[j].astype(BF16)
        else:
            w_main_bf = dn_w_in[j][:, :DN_MAIN].astype(BF16)
            w_ab_bf = jnp.pad(dn_w_in[j][:, DN_MAIN:], ((0, 0), (0, LANES - 4 * DN_HEADS))).astype(BF16)
            w_out_bf = dn_w_out[j].astype(BF16)
            alog_row = jnp.pad(dn_a_log[j].reshape(1, 2 * DN_HEADS), ((0, 0), (0, LANES - 2 * DN_HEADS)))
            dtb_row = jnp.pad(dn_dt_bias[j].reshape(1, 2 * DN_HEADS), ((0, 0), (0, LANES - 2 * DN_HEADS)))
        for name, st in streams.items():
            x = st["x"]
            seq = st["seq"]
            rows = x.shape[0]
            if l % 2 == 0:
                if name == "ctx":
                    q, new_k, new_v = _attn_proj(x, mods_l, st["cond"], norm1[l], w_in_bf, attn_q_gain[j],
                                                 attn_k_gain[j], rope_cos, rope_sin, st["rope"], tm,
                                                 layered=(j, n_attn, new_k, new_v))
                    o = _context_attention(q, new_k, new_v, j, seq)
                else:
                    q, k, v = _attn_proj(x, mods_l, st["cond"], norm1[l], w_in_bf, attn_q_gain[j], attn_k_gain[j],
                                         rope_cos, rope_sin, st["rope"], tm)
                    o = _latent_attention(q, k, v, cache_k4, cache_v4, j, seq, tm)
            else:
                main, gb, gt = _dn_proj(x, mods_l, st["cond"], norm1[l], w_main_bf, w_ab_bf, alog_row, dtb_row, tm)
                if name == "ctx":
                    o, new_states = _delta_core(main, gb, gt, dn_conv[j], dn_norm[j], None, j, seq, DN_HEADS,
                                                n_layers=n_delta, prev_states=new_states)
                else:
                    (o,) = _delta_core(main, gb, gt, dn_conv[j], dn_norm[j], state_delta, j, seq, DN_HEADS // 2)
            st["x"] = _post(x, o, mods_l, st["cond"], norm2[l], w_out_bf, w1_bf, w2_bf, final_gain, tm)

    y_prompt = streams["ctx"]["x"].reshape(n_ctx, seq_ctx, D_MODEL)
    y_sample = streams["lat"]["x"].reshape(n_lat, seq_lat, D_MODEL)
    new_cache_k = new_k.reshape(n_ctx, n_attn, seq_ctx, KV_HEADS, HEAD_DIM)
    new_cache_v = new_v.reshape(n_ctx, n_attn, seq_ctx, KV_HEADS, HEAD_DIM)
    return (y_prompt, y_sample, new_cache_k, new_cache_v, new_states)
```

```python
import functools

import jax
import jax.numpy as jnp
from jax import lax
from jax.experimental import pallas as pl
from jax.experimental.pallas import tpu as pltpu

F32 = jnp.float32
BF16 = jnp.bfloat16

D_MODEL = 1024
DEPTH = 4
GRID_W = 64
HEAD_DIM = 128
N_HEADS = 8
KV_HEADS = 2
GROUP = N_HEADS // KV_HEADS
Q_WIDTH = N_HEADS * HEAD_DIM
KV_WIDTH = KV_HEADS * HEAD_DIM
ROPE_THETA = 10000.0
ROPE_FREQS = HEAD_DIM // 4
DN_HEADS = 8
DN_DK = 128
DN_KEY = DN_HEADS * DN_DK
DN_MAIN = 4 * DN_KEY
CONV_K = 5
CHUNK = 64
MAX_UNROLLED_CHUNKS = 4
CHUNKS_PER_STEP = 2
D_FF = 4 * D_MODEL
N_MOD = 6
EPS = 1e-6
N_COND = 8
LANES = 128
VMEM_LIMIT = 52 * 1024 * 1024


def _params(n_grid):
    return pltpu.CompilerParams(dimension_semantics=("arbitrary",) * n_grid,
                                vmem_limit_bytes=VMEM_LIMIT)


def _const_spec(shape):
    nd = len(shape)
    return pl.BlockSpec(shape, lambda *_: (0,) * nd)


def _dot(a, b):
    return jnp.dot(a, b, preferred_element_type=F32)


def _dot_nt(a, b):
    return lax.dot_general(a, b, (((1,), (1,)), ((), ())), preferred_element_type=F32)


def _silu(x):
    return x * jax.nn.sigmoid(x)


def _rms(x, gain):
    return x * lax.rsqrt(jnp.mean(x * x, axis=-1, keepdims=True) + EPS) * gain


def _mod_kernel(cond_ref, w_ref, b_ref, o_ref):
    s = _silu(cond_ref[...]).astype(BF16)
    o_ref[...] = _dot(s, w_ref[...].astype(BF16)) + b_ref[...]


def _modulation(cond, w_mod, b_mod):
    tn = 1536
    n_out = N_MOD * D_MODEL
    return pl.pallas_call(
        _mod_kernel,
        grid=(DEPTH, n_out // tn),
        in_specs=[
            _const_spec((N_COND, D_MODEL)),
            pl.BlockSpec((None, D_MODEL, tn), lambda l, n: (l, 0, n)),
            pl.BlockSpec((None, 1, tn), lambda l, n: (l, 0, n)),
        ],
        out_specs=pl.BlockSpec((None, N_COND, tn), lambda l, n: (l, 0, n)),
        out_shape=jax.ShapeDtypeStruct((DEPTH, N_COND, n_out), F32),
        compiler_params=_params(2),
        name="modulation",
    )(cond, w_mod, b_mod.reshape(DEPTH, 1, n_out))


def _mods_spec(cond_of_tile):
    return pl.BlockSpec((None, N_MOD, D_MODEL), lambda i: (cond_of_tile(i), 0, 0))


def _attn_proj_kernel(x_ref, mods_ref, gain_ref, w_ref, qg_ref, kg_ref, *rest, use_rope):
    q_out, k_out, v_out = rest[-3:]
    x = x_ref[...]
    h = _rms(x, gain_ref[...]) * (1.0 + mods_ref[1:2, :]) + mods_ref[0:1, :]
    proj = _dot(h.astype(BF16), w_ref[...])
    if use_rope:
        cos = rest[0][...]
        sin = rest[1][...]
        lane = lax.broadcasted_iota(jnp.int32, cos.shape, 1)
        low_half = (lane % (2 * ROPE_FREQS)) < ROPE_FREQS

    def prep(xh, gain):
        y = _rms(xh, gain)
        if not use_rope:
            return y
        partner = jnp.where(low_half, pltpu.roll(y, LANES - ROPE_FREQS, 1), pltpu.roll(y, ROPE_FREQS, 1))
        return y * cos + partner * sin

    for hh in range(N_HEADS):
        sl = slice(hh * HEAD_DIM, (hh + 1) * HEAD_DIM)
        q_out[:, sl] = (prep(proj[:, sl], qg_ref[...]) * (HEAD_DIM ** -0.5)).astype(BF16)
    if len(k_out.shape) == 3:
        for slot in range(1, k_out.shape[0]):
            k_out[slot] = jnp.zeros(k_out.shape[1:], F32)
            v_out[slot] = jnp.zeros(v_out.shape[1:], F32)
        k_out, v_out = k_out.at[0], v_out.at[0]
    for kv in range(KV_HEADS):
        sl = slice(kv * HEAD_DIM, (kv + 1) * HEAD_DIM)
        k_out[:, sl] = prep(proj[:, Q_WIDTH + kv * HEAD_DIM:Q_WIDTH + (kv + 1) * HEAD_DIM], kg_ref[...])
    v_out[...] = proj[:, Q_WIDTH + KV_WIDTH:]


def _attn_proj(x, mods_l, cond_of_tile, gain, w_bf, q_gain, k_gain, rope_cos, rope_sin, rope_of_tile, tm,
               layered=None):
    rows = x.shape[0]
    use_rope = rope_of_tile is not None
    in_specs = [
        pl.BlockSpec((tm, D_MODEL), lambda i: (i, 0)),
        _mods_spec(cond_of_tile),
        _const_spec((1, D_MODEL)),
        _const_spec((D_MODEL, Q_WIDTH + 2 * KV_WIDTH)),
        _const_spec((1, HEAD_DIM)),
        _const_spec((1, HEAD_DIM)),
    ]
    args = [x, mods_l, gain.reshape(1, D_MODEL), w_bf, q_gain.reshape(1, HEAD_DIM), k_gain.reshape(1, HEAD_DIM)]
    if use_rope:
        in_specs += [pl.BlockSpec((tm, HEAD_DIM), lambda i: (rope_of_tile(i), 0)) for _ in range(2)]
        args += [rope_cos, rope_sin]
    aliases = {}
    if layered is None:
        kv_spec = pl.BlockSpec((tm, KV_WIDTH), lambda i: (i, 0))
        kv_shape = jax.ShapeDtypeStruct((rows, KV_WIDTH), F32)
    else:
        layer_j, n_layers, prev_k, prev_v = layered
        if prev_k is None:
            assert layer_j == 0
            kv_spec = pl.BlockSpec((None, n_layers, tm, KV_WIDTH), lambda i: (i, 0, 0, 0))
        else:
            kv_spec = pl.BlockSpec((None, None, tm, KV_WIDTH), lambda i: (i, layer_j, 0, 0))
        kv_shape = jax.ShapeDtypeStruct((rows // tm, n_layers, tm, KV_WIDTH), F32)
        if prev_k is not None:
            aliases = {len(args): 1, len(args) + 1: 2}
            in_specs += [pl.BlockSpec(memory_space=pl.ANY), pl.BlockSpec(memory_space=pl.ANY)]
            args += [prev_k, prev_v]
    return pl.pallas_call(
        functools.partial(_attn_proj_kernel, use_rope=use_rope),
        grid=(rows // tm,),
        in_specs=in_specs,
        out_specs=[pl.BlockSpec((tm, Q_WIDTH), lambda i: (i, 0)), kv_spec, kv_spec],
        out_shape=[jax.ShapeDtypeStruct((rows, Q_WIDTH), BF16), kv_shape, kv_shape],
        input_output_aliases=aliases,
        compiler_params=_params(1),
        name="attn_proj",
    )(*args)


def _attn_kernel(*refs, n_pieces):
    q_ref = refs[0]
    kv_refs = refs[1:1 + 2 * n_pieces]
    o_ref = refs[1 + 2 * n_pieces]
    q = q_ref[...]
    for kv in range(KV_HEADS):
        sl = slice(kv * HEAD_DIM, (kv + 1) * HEAD_DIM)
        ks = [kv_refs[2 * p][:, sl].astype(BF16) for p in range(n_pieces)]
        vs = [kv_refs[2 * p + 1][:, sl].astype(BF16) for p in range(n_pieces)]
        heads = [kv * GROUP + g for g in range(GROUP)]
        scores = [[_dot_nt(q[:, hh * HEAD_DIM:(hh + 1) * HEAD_DIM], k) for k in ks] for hh in heads]
        probs, denoms = [], []
        for ss in scores:
            m = functools.reduce(jnp.maximum, [jnp.max(s, axis=-1, keepdims=True) for s in ss])
            ps = [jnp.exp(s - m) for s in ss]
            denoms.append(functools.reduce(jnp.add, [jnp.sum(p, axis=-1, keepdims=True) for p in ps]))
            probs.append([p.astype(BF16) for p in ps])
        outs = [functools.reduce(jnp.add, [_dot(p, v) for p, v in zip(ps, vs)]) for ps in probs]
        for hh, o, denom in zip(heads, outs, denoms):
            o_ref[:, hh * HEAD_DIM:(hh + 1) * HEAD_DIM] = (o / denom).astype(BF16)


def _context_attention(q, k4, v4, layer_j, seq):
    rows = q.shape[0]
    kv_spec = pl.BlockSpec((None, None, seq, KV_WIDTH), lambda b: (b, layer_j, 0, 0))
    return pl.pallas_call(
        functools.partial(_attn_kernel, n_pieces=1),
        grid=(rows // seq,),
        in_specs=[pl.BlockSpec((seq, Q_WIDTH), lambda b: (b, 0)), kv_spec, kv_spec],
        out_specs=pl.BlockSpec((seq, Q_WIDTH), lambda b: (b, 0)),
        out_shape=jax.ShapeDtypeStruct((rows, Q_WIDTH), BF16),
        compiler_params=_params(1),
        name="context_attention",
    )(q, k4, v4)


def _latent_attention(q, k, v, cache_k4, cache_v4, layer_j, seq, tq):
    rows = q.shape[0]
    past = cache_k4.shape[2]
    nq = seq // tq
    cache_spec = pl.BlockSpec((None, None, past, KV_WIDTH), lambda b, i: (b, layer_j, 0, 0))
    return pl.pallas_call(
        functools.partial(_attn_kernel, n_pieces=2),
        grid=(rows // seq, nq),
        in_specs=[
            pl.BlockSpec((tq, Q_WIDTH), lambda b, i: (b * nq + i, 0)),
            cache_spec,
            cache_spec,
            pl.BlockSpec((seq, KV_WIDTH), lambda b, i: (b, 0)),
            pl.BlockSpec((seq, KV_WIDTH), lambda b, i: (b, 0)),
        ],
        out_specs=pl.BlockSpec((tq, Q_WIDTH), lambda b, i: (b * nq + i, 0)),
        out_shape=jax.ShapeDtypeStruct((rows, Q_WIDTH), BF16),
        compiler_params=_params(2),
        name="latent_attention",
    )(q, cache_k4, cache_v4, k, v)


def _dn_proj_kernel(x_ref, mods_ref, gain_ref, w_ref, wab_ref, alog_ref, dtb_ref, main_out, gb_out, gt_out):
    x = x_ref[...]
    h = (_rms(x, gain_ref[...]) * (1.0 + mods_ref[1:2, :]) + mods_ref[0:1, :]).astype(BF16)
    main_out[...] = _dot(h, w_ref[...])
    ab = _dot(h, wab_ref[...])
    lane = lax.broadcasted_iota(jnp.int32, ab.shape, 1)
    g = -jnp.exp(alog_ref[...]) * jax.nn.softplus(ab + dtb_ref[...])
    tm = ab.shape[0]
    r = lax.broadcasted_iota(jnp.int32, (tm, tm), 0)
    c = lax.broadcasted_iota(jnp.int32, (tm, tm), 1)
    same_chunk = (r // CHUNK) == (c // CHUNK)
    m_fwd = jnp.where(same_chunk, jnp.where(c <= r, 1.0, 0.0), 0.0).astype(BF16)
    m_bwd = jnp.where(same_chunk, jnp.where(c >= r, 1.0, 0.0), 0.0).astype(BF16)
    g1 = g.astype(BF16)
    rem = g - g1.astype(F32)
    g2 = rem.astype(BF16)
    g3 = (rem - g2.astype(F32)).astype(BF16)
    cum_f = _dot(m_fwd, g1) + (_dot(m_fwd, g2) + _dot(m_fwd, g3))
    cum_b = _dot(m_bwd, g1) + (_dot(m_bwd, g2) + _dot(m_bwd, g3))
    total = cum_f + cum_b - g
    out = jnp.where(lane < DN_HEADS, cum_f, jnp.where(lane < 2 * DN_HEADS, cum_b, jax.nn.sigmoid(ab)))
    table = jnp.where(lane < 4 * DN_HEADS, out, pltpu.roll(total, 4 * DN_HEADS, 1))
    gb_out[...] = table
    for ch in range(tm // CHUNK):
        gt_out[ch] = jnp.transpose(table[ch * CHUNK:(ch + 1) * CHUNK, :])


def _dn_proj(x, mods_l, cond_of_tile, gain, w_main_bf, w_ab_bf, alog_row, dtb_row, tm):
    rows = x.shape[0]
    return pl.pallas_call(
        _dn_proj_kernel,
        grid=(rows // tm,),
        in_specs=[
            pl.BlockSpec((tm, D_MODEL), lambda i: (i, 0)),
            _mods_spec(cond_of_tile),
            _const_spec((1, D_MODEL)),
            _const_spec((D_MODEL, DN_MAIN)),
            _const_spec((D_MODEL, LANES)),
            _const_spec((1, LANES)),
            _const_spec((1, LANES)),
        ],
        out_specs=[
            pl.BlockSpec((tm, DN_MAIN), lambda i: (i, 0)),
            pl.BlockSpec((tm, LANES), lambda i: (i, 0)),
            pl.BlockSpec((tm // CHUNK, LANES, CHUNK), lambda i: (i, 0, 0)),
        ],
        out_shape=[
            jax.ShapeDtypeStruct((rows, DN_MAIN), F32),
            jax.ShapeDtypeStruct((rows, LANES), F32),
            jax.ShapeDtypeStruct((rows // CHUNK, LANES, CHUNK), F32),
        ],
        compiler_params=_params(1),
        name="dn_proj",
    )(x, mods_l, gain.reshape(1, D_MODEL), w_main_bf, w_ab_bf, alog_row, dtb_row)


def _split_bf16(a):
    hi = a.astype(BF16)
    lo = (a - hi.astype(F32)).astype(BF16)
    return hi, lo


def _unit_tri_inverse_pair(low, eye, block_diag_mask):
    c = CHUNK

    def block_diag(x_bf):
        return jnp.where(block_diag_mask, jnp.concatenate([x_bf, x_bf], axis=0), jnp.zeros((), BF16))

    p = eye - low
    m = low
    steps = CHUNK.bit_length() - 2
    for it in range(steps + 1):
        need_square = it < steps
        need_pm = it > 0
        mh, ml = _split_bf16(m)
        lhs_a, lhs_b = [], []
        if need_square:
            lhs_a += [mh, ml]
            lhs_b += [mh]
        if need_pm:
            ph, pl_ = _split_bf16(p)
            lhs_a += [ph, pl_]
            lhs_b += [ph]
        a = _dot(jnp.concatenate(lhs_a, axis=0), block_diag(mh))
        b = _dot(jnp.concatenate(lhs_b, axis=0), block_diag(ml))
        off_a = off_b = 0
        if need_square:
            m = a[0:c] + a[c:2 * c] + b[0:c]
            off_a, off_b = 2 * c, c
        if need_pm:
            p = p + (a[off_a:off_a + c] + a[off_a + c:off_a + 2 * c] + b[off_b:off_b + c])
        yield
    return p


def _delta_kernel(*refs, seq, hg, has_s0, has_prev):
    (q_ref, k_ref, v_ref, z_ref, gb_ref, gt_ref, cq_ref, ck_ref, cv_ref, ng_ref) = refs[:10]
    pos = 10
    s0_ref = None
    if has_s0:
        s0_ref = refs[pos]
        pos += 1
    if has_prev:
        pos += 1
    og_ref = refs[pos]
    pos += 1
    st_ref = None
    if not has_s0:
        st_ref = refs[pos]
        pos += 1
    q_s, k_s, v_s, of_s, ob_s, pad_q, pad_k, pad_v, st_s = refs[pos:]

    head0 = pl.program_id(1) * hg
    n_chunks = seq // CHUNK
    halo = 8
    width = hg * DN_DK
    unrolled = n_chunks <= MAX_UNROLLED_CHUNKS

    for pad, x_ref in ((pad_q, q_ref), (pad_k, k_ref), (pad_v, v_ref)):
        pad[0:halo, :] = jnp.zeros((halo, width), F32)
        pad[halo + seq:2 * halo + seq, :] = jnp.zeros((halo, width), F32)
        pad[halo:halo + seq, :] = x_ref[...]

    def prep_chunk(c):
        r = c * CHUNK
        base = halo - CONV_K // 2 + r

        def conv_silu(pad, cw_ref):
            acc = pad[base:base + CHUNK, :] * cw_ref[0:1, :]
            for j in range(1, CONV_K):
                acc = acc + pad[base + j:base + j + CHUNK, :] * cw_ref[j:j + 1, :]
            return _silu(acc)

        def l2n_heads(x, scale, dst):
            for h in range(hg):
                sl = slice(h * DN_DK, (h + 1) * DN_DK)
                xh = x[:, sl]
                dst[r:r + CHUNK, sl] = xh * (lax.rsqrt(jnp.sum(xh * xh, axis=-1, keepdims=True) + EPS) * scale)

        l2n_heads(conv_silu(pad_q, cq_ref), DN_DK ** -0.5, q_s)
        l2n_heads(conv_silu(pad_k, ck_ref), 1.0, k_s)
        v_s[r:r + CHUNK, :] = conv_silu(pad_v, cv_ref)

    order = []
    for c in range(n_chunks):
        for cc in (c, n_chunks - 1 - c):
            if cc not in order:
                order.append(cc)
    for c in order:
        prep_chunk(c)

    for d in range(2):
        for h in range(hg):
            if has_s0:
                st_s[d * hg + h] = s0_ref[d, h]
            else:
                st_s[d * hg + h] = jnp.zeros((DN_DK, DN_DK), F32)

    pair_w = 2 * CHUNK
    row_p = lax.broadcasted_iota(jnp.int32, (CHUNK, pair_w), 0)
    lane_p = lax.broadcasted_iota(jnp.int32, (CHUNK, pair_w), 1)
    col_p = lane_p & (CHUNK - 1)
    in_b = lane_p >= CHUNK
    eye = (row_p == col_p).astype(F32)
    bd_r = lax.broadcasted_iota(jnp.int32, (pair_w, pair_w), 0)
    bd_c = lax.broadcasted_iota(jnp.int32, (pair_w, pair_w), 1)
    block_diag_mask = (bd_r >= CHUNK) == (bd_c >= CHUNK)
    lane = lax.broadcasted_iota(jnp.int32, (CHUNK, LANES), 1)
    zeros_k = jnp.zeros((CHUNK, DN_DK), BF16)
    kq_cache = {}

    def pair_step(n, ha, direction):
        hb = ha + 1
        r0 = n * CHUNK if unrolled else pl.multiple_of(n * CHUNK, CHUNK)
        sls = [slice(h * DN_DK, (h + 1) * DN_DK) for h in (ha, hb)]
        qs = [q_s[pl.ds(r0, CHUNK), sl] for sl in sls]
        ks = [k_s[pl.ds(r0, CHUNK), sl] for sl in sls]
        vs = [v_s[pl.ds(r0, CHUNK), sl].astype(BF16) for sl in sls]
        ks_bf = [k.astype(BF16) for k in ks]
        gbc = gb_ref[pl.ds(r0, CHUNK), :]
        col_a = direction * DN_HEADS + head0 + ha

        def col_vec(c):
            return jnp.sum(jnp.where(lane == c, gbc, 0.0), axis=1, keepdims=True)

        def row_vec(r):
            both = gt_ref[n, pl.ds(r, 2), :]
            return jnp.concatenate([both[0:1], both[1:2]], axis=1)

        gcs = [col_vec(col_a), col_vec(col_a + 1)]
        gc_col = jnp.where(in_b, gcs[1], gcs[0])
        b_col = jnp.where(in_b, col_vec(col_a + 1 + 2 * DN_HEADS), col_vec(col_a + 2 * DN_HEADS))
        gc_row = row_vec(col_a)
        b_row = row_vec(col_a + 2 * DN_HEADS)
        tot = gt_ref[n, pl.ds(col_a + 4 * DN_HEADS, 2), :]
        g_tots = [tot[0:1, 0:1], tot[1:2, 0:1]]
        if direction == 0:
            incl = col_p <= row_p
            strict = col_p < row_p
        else:
            incl = col_p >= row_p
            strict = col_p > row_p
        decay = jnp.exp(jnp.where(incl, gc_col - gc_row, 0.0))
        kq = kq_cache.get((n, ha)) if unrolled else None
        if kq is None:
            kq = (_dot_nt(jnp.concatenate([ks_bf[0], qs[0].astype(BF16)], axis=0),
                          jnp.concatenate([ks_bf[0], zeros_k], axis=0))
                  + _dot_nt(jnp.concatenate([ks_bf[1], qs[1].astype(BF16)], axis=0),
                            jnp.concatenate([zeros_k, ks_bf[1]], axis=0)))
            if unrolled:
                kq_cache[(n, ha)] = kq
        yield
        low = jnp.where(strict, kq[0:CHUNK] * (decay * b_col), 0.0)
        qk = jnp.where(incl, kq[CHUNK:2 * CHUNK] * decay, 0.0).astype(BF16)
        tinv = yield from _unit_tri_inverse_pair(low, eye, block_diag_mask)
        t_u = (tinv * b_row).astype(BF16)
        t_w = (tinv * (b_row * jnp.exp(gc_row))).astype(BF16)
        pick = [lambda x: jnp.concatenate([x, zeros_k], axis=0), lambda x: jnp.concatenate([zeros_k, x], axis=0)]
        us = [_dot(t_u, pick[i](vs[i])) for i in range(2)]
        ws = [_dot(t_w, pick[i](ks_bf[i])) for i in range(2)]
        qgs = [(qs[i] * jnp.exp(gcs[i])).astype(BF16) for i in range(2)]
        kd = jnp.concatenate([ks[i] * jnp.exp(g_tots[i] - gcs[i]) for i in range(2)], axis=0)
        kd_t = jnp.transpose(kd).astype(BF16)
        yield
        return dict(us=us, ws=[w.astype(BF16) for w in ws], qgs=qgs, lhs_od=jnp.concatenate([qk, kd_t], axis=0),
                    decay_tot=[jnp.exp(g) for g in g_tots], pick=pick, r0=r0, sls=sls,
                    dst=of_s if direction == 0 else ob_s)

    def pair_scan(a, states):
        if states is None:
            yield
            v_new = [a["us"][i].astype(BF16) for i in range(2)]
        else:
            ws_qs = [_dot(jnp.concatenate([a["ws"][i], a["qgs"][i]], axis=0), states[i].astype(BF16))
                     for i in range(2)]
            yield
            v_new = [(a["us"][i] - ws_qs[i][0:CHUNK]).astype(BF16) for i in range(2)]
        ods = [_dot(a["lhs_od"], a["pick"][i](v_new[i])) for i in range(2)]
        yield
        if states is None:
            return [ods[i][CHUNK:CHUNK + DN_DK] for i in range(2)], [ods[i][0:CHUNK] for i in range(2)]
        new_states = [states[i] * a["decay_tot"][i] + ods[i][CHUNK:CHUNK + DN_DK] for i in range(2)]
        outs = [ws_qs[i][CHUNK:2 * CHUNK] + ods[i][0:CHUNK] for i in range(2)]
        return new_states, outs

    def round_robin(gens):
        results = [None] * len(gens)
        while any(r is None for r in results):
            for gi, gen in enumerate(gens):
                if results[gi] is None:
                    try:
                        next(gen)
                    except StopIteration as done:
                        results[gi] = done.value
        return results

    lanes_of = [(ha, d) for ha in range(0, hg, 2) for d in range(2)]

    def body(it, carry):
        def chunk_of(s, d):
            c = it * sub + s
            return c if d == 0 else n_chunks - 1 - c

        preps = round_robin([pair_step(chunk_of(s, d), ha, d) for s in range(sub) for ha, d in lanes_of])
        if unrolled and not has_s0 and it == 0:
            states = [None] * len(lanes_of)
        else:
            states = [[st_s[d * hg + ha], st_s[d * hg + ha + 1]] for ha, d in lanes_of]
        stores = []
        for s in range(sub):
            mine = preps[s * len(lanes_of):(s + 1) * len(lanes_of)]
            scanned = round_robin([pair_scan(a, st) for a, st in zip(mine, states)])
            states = [new for new, _ in scanned]
            stores += [(a, outs) for a, (_, outs) in zip(mine, scanned)]
        for (ha, d), st in zip(lanes_of, states):
            st_s[d * hg + ha] = st[0]
            st_s[d * hg + ha + 1] = st[1]
        for a, outs in stores:
            for i in range(2):
                a["dst"][pl.ds(a["r0"], CHUNK), a["sls"][i]] = outs[i]
        return carry

    if unrolled:
        sub = 1
        for n in range(n_chunks):
            body(n, 0)
    else:
        sub = CHUNKS_PER_STEP
        lax.fori_loop(0, n_chunks // sub, body, 0)
    if st_ref is not None:
        if len(st_ref.shape) == 5:
            for slot in range(1, st_ref.shape[0]):
                st_ref[slot] = jnp.zeros(st_ref.shape[1:], F32)
            st_ref = st_ref.at[0]
        for d in range(2):
            for h in range(hg):
                st_ref[d, h] = st_s[d * hg + h]

    for h in range(hg):
        sl = slice(h * DN_DK, (h + 1) * DN_DK)
        o = of_s[:, sl] + ob_s[:, sl]
        og_ref[:, sl] = (_rms(o, ng_ref[...]) * _silu(z_ref[:, sl])).astype(BF16)


def _delta_core(main, gb, gt, conv_w, norm_gain, s0, layer_j, seq, hg, n_layers=None, prev_states=None):
    rows = main.shape[0]
    n_seq = rows // seq
    n_chunks = seq // CHUNK
    has_s0 = s0 is not None
    n_groups = DN_HEADS // hg
    width = hg * DN_DK
    aliases = {}

    def col_spec(group):
        return pl.BlockSpec((seq, width), lambda b, g: (b, group * n_groups + g))

    def conv_spec(group):
        return pl.BlockSpec((CONV_K, width), lambda b, g: (0, group * n_groups + g))

    in_specs = [
        col_spec(0), col_spec(1), col_spec(2), col_spec(3),
        pl.BlockSpec((seq, LANES), lambda b, g: (b, 0)),
        pl.BlockSpec((n_chunks, LANES, CHUNK), lambda b, g: (b, 0, 0)),
        conv_spec(0), conv_spec(1), conv_spec(2),
        pl.BlockSpec((1, DN_DK), lambda b, g: (0, 0)),
    ]
    args = [main, main, main, main, gb, gt, conv_w, conv_w, conv_w, norm_gain.reshape(1, DN_DK)]
    out_specs = [pl.BlockSpec((seq, width), lambda b, g: (b, g))]
    out_shape = [jax.ShapeDtypeStruct((rows, DN_KEY), BF16)]
    if has_s0:
        in_specs.append(pl.BlockSpec((None, None, 2, hg, DN_DK, DN_DK), lambda b, g: (b, layer_j, 0, g, 0, 0)))
        args.append(s0)
    else:
        if prev_states is None:
            assert layer_j == 0
            out_specs.append(pl.BlockSpec((None, n_layers, 2, hg, DN_DK, DN_DK), lambda b, g: (b, 0, 0, g, 0, 0)))
        else:
            out_specs.append(pl.BlockSpec((None, None, 2, hg, DN_DK, DN_DK), lambda b, g: (b, layer_j, 0, g, 0, 0)))
        out_shape.append(jax.ShapeDtypeStruct((n_seq, n_layers, 2, DN_HEADS, DN_DK, DN_DK), F32))
        if prev_states is not None:
            aliases = {len(args): 1}
            in_specs.append(pl.BlockSpec(memory_space=pl.ANY))
            args.append(prev_states)
    scratch = [pltpu.VMEM((seq, width), F32) for _ in range(5)]
    scratch += [pltpu.VMEM((seq + 16, width), F32) for _ in range(3)]
    scratch.append(pltpu.VMEM((2 * hg, DN_DK, DN_DK), F32))
    return pl.pallas_call(
        functools.partial(_delta_kernel, seq=seq, hg=hg, has_s0=has_s0, has_prev=bool(aliases)),
        grid=(n_seq, n_groups),
        in_specs=in_specs,
        out_specs=out_specs,
        out_shape=out_shape,
        scratch_shapes=scratch,
        input_output_aliases=aliases,
        compiler_params=_params(2),
        name="delta_core",
    )(*args)


def _post_kernel(*refs, final, ff_chunk):
    x_ref, o_ref, mods_ref, g2_ref, wout_ref, w1_ref, w2_ref = refs[:7]
    fin_ref = refs[7] if final else None
    out_ref = refs[-1]
    x1 = x_ref[...] + mods_ref[2:3, :] * _dot(o_ref[...], wout_ref[...])
    h = (_rms(x1, g2_ref[...]) * (1.0 + mods_ref[4:5, :]) + mods_ref[3:4, :]).astype(BF16)
    acc = jnp.zeros(x1.shape, F32)
    for c in range(D_FF // ff_chunk):
        a = _dot(h, w1_ref[:, c * ff_chunk:(c + 1) * ff_chunk])
        a = jnp.square(jnp.maximum(a, 0.0)).astype(BF16)
        acc = acc + _dot(a, w2_ref[c * ff_chunk:(c + 1) * ff_chunk, :])
    x2 = x1 + mods_ref[5:6, :] * acc
    if final:
        x2 = _rms(x2, fin_ref[...])
    out_ref[...] = x2


def _post(x, o, mods_l, cond_of_tile, gain2, wout_bf, w1_bf, w2_bf, final_gain, tm):
    rows = x.shape[0]
    final = final_gain is not None
    in_specs = [
        pl.BlockSpec((tm, D_MODEL), lambda i: (i, 0)),
        pl.BlockSpec((tm, D_MODEL), lambda i: (i, 0)),
        _mods_spec(cond_of_tile),
        _const_spec((1, D_MODEL)),
        _const_spec((D_MODEL, D_MODEL)),
        _const_spec((D_MODEL, D_FF)),
        _const_spec((D_FF, D_MODEL)),
    ]
    args = [x, o, mods_l, gain2.reshape(1, D_MODEL), wout_bf, w1_bf, w2_bf]
    if final:
        in_specs.append(_const_spec((1, D_MODEL)))
        args.append(final_gain.reshape(1, D_MODEL))
    return pl.pallas_call(
        functools.partial(_post_kernel, final=final, ff_chunk=1024),
        grid=(rows // tm,),
        in_specs=in_specs,
        out_specs=pl.BlockSpec((tm, D_MODEL), lambda i: (i, 0)),
        out_shape=jax.ShapeDtypeStruct((rows, D_MODEL), F32),
        compiler_params=_params(1),
        name="post_mlp",
    )(*args)


def _rope_tables(n_tokens):
    rows = n_tokens // GRID_W
    row = jnp.repeat(jnp.arange(rows), GRID_W).astype(F32)
    col = jnp.tile(jnp.arange(GRID_W), rows).astype(F32)
    inv = ROPE_THETA ** (-jnp.arange(ROPE_FREQS, dtype=F32) / ROPE_FREQS)
    ang_r = row[:, None] * inv
    ang_c = col[:, None] * inv
    cos = jnp.concatenate([jnp.cos(ang_r), jnp.cos(ang_r), jnp.cos(ang_c), jnp.cos(ang_c)], axis=1)
    sin = jnp.concatenate([-jnp.sin(ang_r), jnp.sin(ang_r), -jnp.sin(ang_c), jnp.sin(ang_c)], axis=1)
    return cos, sin


def kernel(x_prompt, x_sample, cache_k, cache_v, state_delta, c, c_ctx, w_mod, b_mod, norm1, norm2,
           attn_w_in, attn_q_gain, attn_k_gain, attn_w_out, dn_w_in, dn_conv, dn_a_log, dn_dt_bias,
           dn_norm, dn_w_out, mlp_w1, mlp_w2, final_norm):
    n_ctx, seq_ctx, _ = x_prompt.shape
    n_lat, seq_lat, _ = x_sample.shape
    past = cache_k.shape[2]
    tm = 256

    cond = jnp.concatenate([c_ctx[None, :], c, jnp.zeros((N_COND - 1 - n_lat, D_MODEL), F32)], axis=0)
    mods = _modulation(cond, w_mod, b_mod).reshape(DEPTH, N_COND, N_MOD, D_MODEL)

    rope_cos, rope_sin = _rope_tables(seq_lat)
    lat_tiles = seq_lat // tm
    cache_k4 = cache_k.reshape(n_lat, cache_k.shape[1], past, KV_WIDTH)
    cache_v4 = cache_v.reshape(n_lat, cache_v.shape[1], past, KV_WIDTH)

    streams = {
        "ctx": dict(x=x_prompt.reshape(n_ctx * seq_ctx, D_MODEL), seq=seq_ctx,
                    cond=lambda i: 0, rope=None),
        "lat": dict(x=x_sample.reshape(n_lat * seq_lat, D_MODEL), seq=seq_lat,
                    cond=lambda i: 1 + i // lat_tiles, rope=lambda i: i % lat_tiles),
    }
    n_attn = (DEPTH + 1) // 2
    n_delta = DEPTH // 2
    new_k = new_v = new_states = None
    assert tm == seq_ctx

    for l in range(DEPTH):
        j = l // 2
        mods_l = mods[l]
        w1_bf = mlp_w1[l].astype(BF16)
        w2_bf = mlp_w2[l].astype(BF16)
        final_gain = final_norm if l == DEPTH - 1 else None
        if l % 2 == 0:
            w_in_bf = attn_w_in[j].astype(BF16)
            w_out_bf = attn_w_out[j].astype(BF16)
        else:
            w_main_bf = dn_w_in[j][:, :DN_MAIN].astype(BF16)
            w_ab_bf = jnp.pad(dn_w_in[j][:, DN_MAIN:], ((0, 0), (0, LANES - 4 * DN_HEADS))).astype(BF16)
            w_out_bf = dn_w_out[j].astype(BF16)
            alog_row = jnp.pad(dn_a_log[j].reshape(1, 2 * DN_HEADS), ((0, 0), (0, LANES - 2 * DN_HEADS)))
            dtb_row = jnp.pad(dn_dt_bias[j].reshape(1, 2 * DN_HEADS), ((0, 0), (0, LANES - 2 * DN_HEADS)))
        for name, st in streams.items():
            x = st["x"]
            seq = st["seq"]
            rows = x.shape[0]
            if l % 2 == 0:
                if name == "ctx":
                    q, new_k, new_v = _attn_proj(x, mods_l, st["cond"], norm1[l], w_in_bf, attn_q_gain[j],
                                                 attn_k_gain[j], rope_cos, rope_sin, st["rope"], tm,
                                                 layered=(j, n_attn, new_k, new_v))
                    o = _context_attention(q, new_k, new_v, j, seq)
                else:
                    q, k, v = _attn_proj(x, mods_l, st["cond"], norm1[l], w_in_bf, attn_q_gain[j], attn_k_gain[j],
                                         rope_cos, rope_sin, st["rope"], tm)
                    o = _latent_attention(q, k, v, cache_k4, cache_v4, j, seq, tm)
            else:
                main, gb, gt = _dn_proj(x, mods_l, st["cond"], norm1[l], w_main_bf, w_ab_bf, alog_row, dtb_row, tm)
                if name == "ctx":
                    o, new_states = _delta_core(main, gb, gt, dn_conv[j], dn_norm[j], None, j, seq, DN_HEADS,
                                                n_layers=n_delta, prev_states=new_states)
                else:
                    (o,) = _delta_core(main, gb, gt, dn_conv[j], dn_norm[j], state_delta, j, seq, DN_HEADS // 2)
            st["x"] = _post(x, o, mods_l, st["cond"], norm2[l], w_out_bf, w1_bf, w2_bf, final_gain, tm)

    y_prompt = streams["ctx"]["x"].reshape(n_ctx, seq_ctx, D_MODEL)
    y_sample = streams["lat"]["x"].reshape(n_lat, seq_lat, D_MODEL)
    new_cache_k = new_k.reshape(n_ctx, n_attn, seq_ctx, KV_HEADS, HEAD_DIM)
    new_cache_v = new_v.reshape(n_ctx, n_attn, seq_ctx, KV_HEADS, HEAD_DIM)
    return (y_prompt, y_sample, new_cache_k, new_cache_v, new_states)
```

```python
import functools

import jax
import jax.numpy as jnp
from jax import lax
from jax.experimental import pallas as pl
from jax.experimental.pallas import tpu as pltpu

F32 = jnp.float32
BF16 = jnp.bfloat16

D_MODEL = 1024
DEPTH = 4
GRID_W = 64
HEAD_DIM = 128
N_HEADS = 8
KV_HEADS = 2
GROUP = N_HEADS // KV_HEADS
Q_WIDTH = N_HEADS * HEAD_DIM
KV_WIDTH = KV_HEADS * HEAD_DIM
ROPE_THETA = 10000.0
ROPE_FREQS = HEAD_DIM // 4
DN_HEADS = 8
DN_DK = 128
DN_KEY = DN_HEADS * DN_DK
DN_MAIN = 4 * DN_KEY
CONV_K = 5
CHUNK = 64
MAX_UNROLLED_CHUNKS = 4
CHUNKS_PER_STEP = 2
D_FF = 4 * D_MODEL
N_MOD = 6
EPS = 1e-6
N_COND = 8
LANES = 128
VMEM_LIMIT = 52 * 1024 * 1024


def _params(n_grid):
    return pltpu.CompilerParams(dimension_semantics=("arbitrary",) * n_grid,
                                vmem_limit_bytes=VMEM_LIMIT)


def _const_spec(shape):
    nd = len(shape)
    return pl.BlockSpec(shape, lambda *_: (0,) * nd)


def _resident_spec(shape):
    nd = len(shape)
    return pl.BlockSpec(shape, lambda *_: (0,) * nd, pipeline_mode=pl.Buffered(1))


def _dot(a, b):
    return jnp.dot(a, b, preferred_element_type=F32)


def _dot_nt(a, b):
    return lax.dot_general(a, b, (((1,), (1,)), ((), ())), preferred_element_type=F32)


def _silu(x):
    return x * jax.nn.sigmoid(x)


def _rms(x, gain):
    return x * lax.rsqrt(jnp.mean(x * x, axis=-1, keepdims=True) + EPS) * gain


def _mod_kernel(cond_ref, w_ref, b_ref, o_ref):
    s = _silu(cond_ref[...]).astype(BF16)
    o_ref[...] = _dot(s, w_ref[...].astype(BF16)) + b_ref[...]


def _modulation(cond, w_mod, b_mod):
    tn = 1536
    n_out = N_MOD * D_MODEL
    return pl.pallas_call(
        _mod_kernel,
        grid=(DEPTH, n_out // tn),
        in_specs=[
            _const_spec((N_COND, D_MODEL)),
            pl.BlockSpec((None, D_MODEL, tn), lambda l, n: (l, 0, n)),
            pl.BlockSpec((None, 1, tn), lambda l, n: (l, 0, n)),
        ],
        out_specs=pl.BlockSpec((None, N_COND, tn), lambda l, n: (l, 0, n)),
        out_shape=jax.ShapeDtypeStruct((DEPTH, N_COND, n_out), F32),
        compiler_params=_params(2),
        name="modulation",
    )(cond, w_mod, b_mod.reshape(DEPTH, 1, n_out))


def _mods_spec(cond_of_tile):
    return pl.BlockSpec((None, N_MOD, D_MODEL), lambda i: (cond_of_tile(i), 0, 0))


def _attn_proj_kernel(x_ref, mods_ref, gain_ref, w_ref, qg_ref, kg_ref, *rest, use_rope):
    q_out, k_out, v_out = rest[-3:]
    x = x_ref[...]
    h = _rms(x, gain_ref[...]) * (1.0 + mods_ref[1:2, :]) + mods_ref[0:1, :]
    proj = _dot(h.astype(BF16), w_ref[...])
    if use_rope:
        cos = rest[0][...]
        sin = rest[1][...]
        lane = lax.broadcasted_iota(jnp.int32, cos.shape, 1)
        low_half = (lane % (2 * ROPE_FREQS)) < ROPE_FREQS

    def prep(xh, gain):
        y = _rms(xh, gain)
        if not use_rope:
            return y
        partner = jnp.where(low_half, pltpu.roll(y, LANES - ROPE_FREQS, 1), pltpu.roll(y, ROPE_FREQS, 1))
        return y * cos + partner * sin

    for hh in range(N_HEADS):
        sl = slice(hh * HEAD_DIM, (hh + 1) * HEAD_DIM)
        q_out[:, sl] = (prep(proj[:, sl], qg_ref[...]) * (HEAD_DIM ** -0.5)).astype(BF16)
    if len(k_out.shape) == 3:
        for slot in range(1, k_out.shape[0]):
            k_out[slot] = jnp.zeros(k_out.shape[1:], F32)
            v_out[slot] = jnp.zeros(v_out.shape[1:], F32)
        k_out, v_out = k_out.at[0], v_out.at[0]
    for kv in range(KV_HEADS):
        sl = slice(kv * HEAD_DIM, (kv + 1) * HEAD_DIM)
        k_out[:, sl] = prep(proj[:, Q_WIDTH + kv * HEAD_DIM:Q_WIDTH + (kv + 1) * HEAD_DIM], kg_ref[...])
    v_out[...] = proj[:, Q_WIDTH + KV_WIDTH:]


def _attn_proj(x, mods_l, cond_of_tile, gain, w_bf, q_gain, k_gain, rope_cos, rope_sin, rope_of_tile, tm,
               layered=None):
    rows = x.shape[0]
    use_rope = rope_of_tile is not None
    in_specs = [
        pl.BlockSpec((tm, D_MODEL), lambda i: (i, 0)),
        _mods_spec(cond_of_tile),
        _const_spec((1, D_MODEL)),
        _const_spec((D_MODEL, Q_WIDTH + 2 * KV_WIDTH)),
        _const_spec((1, HEAD_DIM)),
        _const_spec((1, HEAD_DIM)),
    ]
    args = [x, mods_l, gain.reshape(1, D_MODEL), w_bf, q_gain.reshape(1, HEAD_DIM), k_gain.reshape(1, HEAD_DIM)]
    if use_rope:
        in_specs += [pl.BlockSpec((tm, HEAD_DIM), lambda i: (rope_of_tile(i), 0)) for _ in range(2)]
        args += [rope_cos, rope_sin]
    aliases = {}
    if layered is None:
        kv_spec = pl.BlockSpec((tm, KV_WIDTH), lambda i: (i, 0))
        kv_shape = jax.ShapeDtypeStruct((rows, KV_WIDTH), F32)
    else:
        layer_j, n_layers, prev_k, prev_v = layered
        if prev_k is None:
            assert layer_j == 0
            kv_spec = pl.BlockSpec((None, n_layers, tm, KV_WIDTH), lambda i: (i, 0, 0, 0))
        else:
            kv_spec = pl.BlockSpec((None, None, tm, KV_WIDTH), lambda i: (i, layer_j, 0, 0))
        kv_shape = jax.ShapeDtypeStruct((rows // tm, n_layers, tm, KV_WIDTH), F32)
        if prev_k is not None:
            aliases = {len(args): 1, len(args) + 1: 2}
            in_specs += [pl.BlockSpec(memory_space=pl.ANY), pl.BlockSpec(memory_space=pl.ANY)]
            args += [prev_k, prev_v]
    return pl.pallas_call(
        functools.partial(_attn_proj_kernel, use_rope=use_rope),
        grid=(rows // tm,),
        in_specs=in_specs,
        out_specs=[pl.BlockSpec((tm, Q_WIDTH), lambda i: (i, 0)), kv_spec, kv_spec],
        out_shape=[jax.ShapeDtypeStruct((rows, Q_WIDTH), BF16), kv_shape, kv_shape],
        input_output_aliases=aliases,
        compiler_params=_params(1),
        name="attn_proj",
    )(*args)


def _attn_kernel(*refs, n_pieces):
    q_ref = refs[0]
    kv_refs = refs[1:1 + 2 * n_pieces]
    o_ref = refs[1 + 2 * n_pieces]
    q = q_ref[...]
    for kv in range(KV_HEADS):
        sl = slice(kv * HEAD_DIM, (kv + 1) * HEAD_DIM)
        ks = [kv_refs[2 * p][:, sl].astype(BF16) for p in range(n_pieces)]
        vs = [kv_refs[2 * p + 1][:, sl].astype(BF16) for p in range(n_pieces)]
        heads = [kv * GROUP + g for g in range(GROUP)]
        scores = [[_dot_nt(q[:, hh * HEAD_DIM:(hh + 1) * HEAD_DIM], k) for k in ks] for hh in heads]
        probs, denoms = [], []
        for ss in scores:
            m = functools.reduce(jnp.maximum, [jnp.max(s, axis=-1, keepdims=True) for s in ss])
            ps = [jnp.exp(s - m) for s in ss]
            denoms.append(functools.reduce(jnp.add, [jnp.sum(p, axis=-1, keepdims=True) for p in ps]))
            probs.append([p.astype(BF16) for p in ps])
        outs = [functools.reduce(jnp.add, [_dot(p, v) for p, v in zip(ps, vs)]) for ps in probs]
        for hh, o, denom in zip(heads, outs, denoms):
            o_ref[:, hh * HEAD_DIM:(hh + 1) * HEAD_DIM] = (o / denom).astype(BF16)


def _context_attention(q, k4, v4, layer_j, seq):
    rows = q.shape[0]
    kv_spec = pl.BlockSpec((None, None, seq, KV_WIDTH), lambda b: (b, layer_j, 0, 0))
    return pl.pallas_call(
        functools.partial(_attn_kernel, n_pieces=1),
        grid=(rows // seq,),
        in_specs=[pl.BlockSpec((seq, Q_WIDTH), lambda b: (b, 0)), kv_spec, kv_spec],
        out_specs=pl.BlockSpec((seq, Q_WIDTH), lambda b: (b, 0)),
        out_shape=jax.ShapeDtypeStruct((rows, Q_WIDTH), BF16),
        compiler_params=_params(1),
        name="context_attention",
    )(q, k4, v4)


def _latent_attention(q, k, v, cache_k4, cache_v4, layer_j, seq, tq):
    rows = q.shape[0]
    past = cache_k4.shape[2]
    nq = seq // tq
    cache_spec = pl.BlockSpec((None, None, past, KV_WIDTH), lambda b, i: (b, layer_j, 0, 0))
    return pl.pallas_call(
        functools.partial(_attn_kernel, n_pieces=2),
        grid=(rows // seq, nq),
        in_specs=[
            pl.BlockSpec((tq, Q_WIDTH), lambda b, i: (b * nq + i, 0)),
            cache_spec,
            cache_spec,
            pl.BlockSpec((seq, KV_WIDTH), lambda b, i: (b, 0)),
            pl.BlockSpec((seq, KV_WIDTH), lambda b, i: (b, 0)),
        ],
        out_specs=pl.BlockSpec((tq, Q_WIDTH), lambda b, i: (b * nq + i, 0)),
        out_shape=jax.ShapeDtypeStruct((rows, Q_WIDTH), BF16),
        compiler_params=_params(2),
        name="latent_attention",
    )(q, cache_k4, cache_v4, k, v)


def _dn_proj_kernel(x_ref, mods_ref, gain_ref, w_ref, wab_ref, alog_ref, dtb_ref, main_out, gb_out, gt_out):
    x = x_ref[...]
    h = (_rms(x, gain_ref[...]) * (1.0 + mods_ref[1:2, :]) + mods_ref[0:1, :]).astype(BF16)
    ab = _dot(h, wab_ref[...])
    main_out[...] = _dot(h, w_ref[...])
    lane = lax.broadcasted_iota(jnp.int32, ab.shape, 1)
    g = -jnp.exp(alog_ref[...]) * jax.nn.softplus(ab + dtb_ref[...])
    tm = ab.shape[0]
    r = lax.broadcasted_iota(jnp.int32, (tm, tm), 0)
    c = lax.broadcasted_iota(jnp.int32, (tm, tm), 1)
    same_chunk = (r // CHUNK) == (c // CHUNK)
    m_fwd = jnp.where(same_chunk, jnp.where(c <= r, 1.0, 0.0), 0.0).astype(BF16)
    m_bwd = jnp.where(same_chunk, jnp.where(c >= r, 1.0, 0.0), 0.0).astype(BF16)
    g1 = g.astype(BF16)
    rem = g - g1.astype(F32)
    g2 = rem.astype(BF16)
    g3 = (rem - g2.astype(F32)).astype(BF16)
    cum_f = _dot(m_fwd, g1) + (_dot(m_fwd, g2) + _dot(m_fwd, g3))
    cum_b = _dot(m_bwd, g1) + (_dot(m_bwd, g2) + _dot(m_bwd, g3))
    total = cum_f + cum_b - g
    out = jnp.where(lane < DN_HEADS, cum_f, jnp.where(lane < 2 * DN_HEADS, cum_b, jax.nn.sigmoid(ab)))
    table = jnp.where(lane < 4 * DN_HEADS, out, pltpu.roll(total, 4 * DN_HEADS, 1))
    gb_out[...] = table
    for ch in range(tm // CHUNK):
        gt_out[ch] = jnp.transpose(table[ch * CHUNK:(ch + 1) * CHUNK, :])


def _dn_proj(x, mods_l, cond_of_tile, gain, w_main_bf, w_ab_bf, alog_row, dtb_row, tm):
    rows = x.shape[0]
    return pl.pallas_call(
        _dn_proj_kernel,
        grid=(rows // tm,),
        in_specs=[
            pl.BlockSpec((tm, D_MODEL), lambda i: (i, 0)),
            _mods_spec(cond_of_tile),
            _const_spec((1, D_MODEL)),
            _const_spec((D_MODEL, DN_MAIN)),
            _const_spec((D_MODEL, LANES)),
            _const_spec((1, LANES)),
            _const_spec((1, LANES)),
        ],
        out_specs=[
            pl.BlockSpec((tm, DN_MAIN), lambda i: (i, 0)),
            pl.BlockSpec((tm, LANES), lambda i: (i, 0)),
            pl.BlockSpec((tm // CHUNK, LANES, CHUNK), lambda i: (i, 0, 0)),
        ],
        out_shape=[
            jax.ShapeDtypeStruct((rows, DN_MAIN), F32),
            jax.ShapeDtypeStruct((rows, LANES), F32),
            jax.ShapeDtypeStruct((rows // CHUNK, LANES, CHUNK), F32),
        ],
        compiler_params=_params(1),
        name="dn_proj",
    )(x, mods_l, gain.reshape(1, D_MODEL), w_main_bf, w_ab_bf, alog_row, dtb_row)


def _split_bf16(a):
    hi = a.astype(BF16)
    lo = (a - hi.astype(F32)).astype(BF16)
    return hi, lo


def _unit_tri_inverse_pair(low, eye, block_diag_mask):
    c = CHUNK

    def block_diag(x_bf):
        return jnp.where(block_diag_mask, jnp.concatenate([x_bf, x_bf], axis=0), jnp.zeros((), BF16))

    p = eye - low
    m = low
    steps = CHUNK.bit_length() - 2
    for it in range(steps + 1):
        need_square = it < steps
        need_pm = it > 0
        mh, ml = _split_bf16(m)
        lhs_a, lhs_b = [], []
        if need_square:
            lhs_a += [mh, ml]
            lhs_b += [mh]
        if need_pm:
            ph, pl_ = _split_bf16(p)
            lhs_a += [ph, pl_]
            lhs_b += [ph]
        a = _dot(jnp.concatenate(lhs_a, axis=0), block_diag(mh))
        b = _dot(jnp.concatenate(lhs_b, axis=0), block_diag(ml))
        off_a = off_b = 0
        if need_square:
            m = a[0:c] + a[c:2 * c] + b[0:c]
            off_a, off_b = 2 * c, c
        if need_pm:
            p = p + (a[off_a:off_a + c] + a[off_a + c:off_a + 2 * c] + b[off_b:off_b + c])
        yield
    return p


def _delta_kernel(*refs, seq, hg, has_s0, has_prev):
    (q_ref, k_ref, v_ref, z_ref, gb_ref, gt_ref, cq_ref, ck_ref, cv_ref, ng_ref) = refs[:10]
    pos = 10
    s0_ref = None
    if has_s0:
        s0_ref = refs[pos]
        pos += 1
    if has_prev:
        pos += 1
    og_ref = refs[pos]
    pos += 1
    st_ref = None
    if not has_s0:
        st_ref = refs[pos]
        pos += 1
    q_s, k_s, v_s, of_s, ob_s, pad_q, pad_k, pad_v, st_s = refs[pos:]

    head0 = pl.program_id(1) * hg
    n_chunks = seq // CHUNK
    halo = 8
    width = hg * DN_DK
    unrolled = n_chunks <= MAX_UNROLLED_CHUNKS

    for pad, x_ref in ((pad_q, q_ref), (pad_k, k_ref), (pad_v, v_ref)):
        pad[0:halo, :] = jnp.zeros((halo, width), F32)
        pad[halo + seq:2 * halo + seq, :] = jnp.zeros((halo, width), F32)
        pad[halo:halo + seq, :] = x_ref[...]

    def prep_chunk(c):
        r = c * CHUNK
        base = halo - CONV_K // 2 + r

        def conv_silu(pad, cw_ref):
            win = pad[r:r + CHUNK + 2 * halo, :]
            n_win = CHUNK + 2 * halo
            acc = None
            for j in range(CONV_K):
                shift = (CONV_K // 2 - j) % n_win
                tap = win if shift == 0 else pltpu.roll(win, shift, 0)
                term = tap[halo:halo + CHUNK, :] * cw_ref[j:j + 1, :]
                acc = term if acc is None else acc + term
            return _silu(acc)

        def l2n_heads(x, scale, dst):
            for h in range(hg):
                sl = slice(h * DN_DK, (h + 1) * DN_DK)
                xh = x[:, sl]
                dst[r:r + CHUNK, sl] = xh * (lax.rsqrt(jnp.sum(xh * xh, axis=-1, keepdims=True) + EPS) * scale)

        l2n_heads(conv_silu(pad_q, cq_ref), DN_DK ** -0.5, q_s)
        l2n_heads(conv_silu(pad_k, ck_ref), 1.0, k_s)
        v_s[r:r + CHUNK, :] = conv_silu(pad_v, cv_ref)

    order = []
    for c in range(n_chunks):
        for cc in (c, n_chunks - 1 - c):
            if cc not in order:
                order.append(cc)
    for c in order:
        prep_chunk(c)

    for d in range(2):
        for h in range(hg):
            if has_s0:
                st_s[d * hg + h] = s0_ref[d, h]
            else:
                st_s[d * hg + h] = jnp.zeros((DN_DK, DN_DK), F32)

    pair_w = 2 * CHUNK
    row_p = lax.broadcasted_iota(jnp.int32, (CHUNK, pair_w), 0)
    lane_p = lax.broadcasted_iota(jnp.int32, (CHUNK, pair_w), 1)
    col_p = lane_p & (CHUNK - 1)
    in_b = lane_p >= CHUNK
    eye = (row_p == col_p).astype(F32)
    bd_r = lax.broadcasted_iota(jnp.int32, (pair_w, pair_w), 0)
    bd_c = lax.broadcasted_iota(jnp.int32, (pair_w, pair_w), 1)
    block_diag_mask = (bd_r >= CHUNK) == (bd_c >= CHUNK)
    lane = lax.broadcasted_iota(jnp.int32, (CHUNK, LANES), 1)
    zeros_k = jnp.zeros((CHUNK, DN_DK), BF16)
    kq_cache = {}

    def pair_step(n, ha, direction):
        hb = ha + 1
        r0 = n * CHUNK if unrolled else pl.multiple_of(n * CHUNK, CHUNK)
        sls = [slice(h * DN_DK, (h + 1) * DN_DK) for h in (ha, hb)]
        qs = [q_s[pl.ds(r0, CHUNK), sl] for sl in sls]
        ks = [k_s[pl.ds(r0, CHUNK), sl] for sl in sls]
        vs = [v_s[pl.ds(r0, CHUNK), sl].astype(BF16) for sl in sls]
        ks_bf = [k.astype(BF16) for k in ks]
        gbc = gb_ref[pl.ds(r0, CHUNK), :]
        col_a = direction * DN_HEADS + head0 + ha

        def col_vec(c):
            return jnp.sum(jnp.where(lane == c, gbc, 0.0), axis=1, keepdims=True)

        def row_vec(r):
            both = gt_ref[n, pl.ds(r, 2), :]
            return jnp.concatenate([both[0:1], both[1:2]], axis=1)

        gcs = [col_vec(col_a), col_vec(col_a + 1)]
        gc_col = jnp.where(in_b, gcs[1], gcs[0])
        b_col = jnp.where(in_b, col_vec(col_a + 1 + 2 * DN_HEADS), col_vec(col_a + 2 * DN_HEADS))
        gc_row = row_vec(col_a)
        b_row = row_vec(col_a + 2 * DN_HEADS)
        tot = gt_ref[n, pl.ds(col_a + 4 * DN_HEADS, 2), :]
        g_tots = [tot[0:1, 0:1], tot[1:2, 0:1]]
        if direction == 0:
            incl = col_p <= row_p
            strict = col_p < row_p
        else:
            incl = col_p >= row_p
            strict = col_p > row_p
        decay = jnp.exp(jnp.where(incl, gc_col - gc_row, 0.0))
        kq = kq_cache.get((n, ha)) if unrolled else None
        if kq is None:
            kq = (_dot_nt(jnp.concatenate([ks_bf[0], qs[0].astype(BF16)], axis=0),
                          jnp.concatenate([ks_bf[0], zeros_k], axis=0))
                  + _dot_nt(jnp.concatenate([ks_bf[1], qs[1].astype(BF16)], axis=0),
                            jnp.concatenate([zeros_k, ks_bf[1]], axis=0)))
            if unrolled:
                kq_cache[(n, ha)] = kq
        yield
        low = jnp.where(strict, kq[0:CHUNK] * (decay * b_col), 0.0)
        qk = jnp.where(incl, kq[CHUNK:2 * CHUNK] * decay, 0.0).astype(BF16)
        tinv = yield from _unit_tri_inverse_pair(low, eye, block_diag_mask)
        t_u = (tinv * b_row).astype(BF16)
        t_w = (tinv * (b_row * jnp.exp(gc_row))).astype(BF16)
        pick = [lambda x: jnp.concatenate([x, zeros_k], axis=0), lambda x: jnp.concatenate([zeros_k, x], axis=0)]
        us = [_dot(t_u, pick[i](vs[i])) for i in range(2)]
        ws = [_dot(t_w, pick[i](ks_bf[i])) for i in range(2)]
        qgs = [(qs[i] * jnp.exp(gcs[i])).astype(BF16) for i in range(2)]
        kd = jnp.concatenate([ks[i] * jnp.exp(g_tots[i] - gcs[i]) for i in range(2)], axis=0)
        kd_t = jnp.transpose(kd).astype(BF16)
        yield
        return dict(us=us, ws=[w.astype(BF16) for w in ws], qgs=qgs, lhs_od=jnp.concatenate([qk, kd_t], axis=0),
                    decay_tot=[jnp.exp(g) for g in g_tots], pick=pick, r0=r0, sls=sls,
                    dst=of_s if direction == 0 else ob_s)

    def pair_scan(a, states):
        if states is None:
            yield
            v_new = [a["us"][i].astype(BF16) for i in range(2)]
        else:
            ws_qs = [_dot(jnp.concatenate([a["ws"][i], a["qgs"][i]], axis=0), states[i].astype(BF16))
                     for i in range(2)]
            yield
            v_new = [(a["us"][i] - ws_qs[i][0:CHUNK]).astype(BF16) for i in range(2)]
        ods = [_dot(a["lhs_od"], a["pick"][i](v_new[i])) for i in range(2)]
        yield
        if states is None:
            return [ods[i][CHUNK:CHUNK + DN_DK] for i in range(2)], [ods[i][0:CHUNK] for i in range(2)]
        new_states = [states[i] * a["decay_tot"][i] + ods[i][CHUNK:CHUNK + DN_DK] for i in range(2)]
        outs = [ws_qs[i][CHUNK:2 * CHUNK] + ods[i][0:CHUNK] for i in range(2)]
        return new_states, outs

    def round_robin(gens):
        results = [None] * len(gens)
        while any(r is None for r in results):
            for gi, gen in enumerate(gens):
                if results[gi] is None:
                    try:
                        next(gen)
                    except StopIteration as done:
                        results[gi] = done.value
        return results

    lanes_of = [(ha, d) for ha in range(0, hg, 2) for d in range(2)]

    def body(it, carry):
        def chunk_of(s, d):
            c = it * sub + s
            return c if d == 0 else n_chunks - 1 - c

        preps = round_robin([pair_step(chunk_of(s, d), ha, d) for s in range(sub) for ha, d in lanes_of])
        if unrolled and not has_s0 and it == 0:
            states = [None] * len(lanes_of)
        else:
            states = [[st_s[d * hg + ha], st_s[d * hg + ha + 1]] for ha, d in lanes_of]
        stores = []
        for s in range(sub):
            mine = preps[s * len(lanes_of):(s + 1) * len(lanes_of)]
            scanned = round_robin([pair_scan(a, st) for a, st in zip(mine, states)])
            states = [new for new, _ in scanned]
            stores += [(a, outs) for a, (_, outs) in zip(mine, scanned)]
        for (ha, d), st in zip(lanes_of, states):
            st_s[d * hg + ha] = st[0]
            st_s[d * hg + ha + 1] = st[1]
        for a, outs in stores:
            for i in range(2):
                a["dst"][pl.ds(a["r0"], CHUNK), a["sls"][i]] = outs[i]
        return carry

    if unrolled:
        sub = 1
        for n in range(n_chunks):
            body(n, 0)
    else:
        sub = CHUNKS_PER_STEP
        lax.fori_loop(0, n_chunks // sub, body, 0)
    if st_ref is not None:
        if len(st_ref.shape) == 5:
            for slot in range(1, st_ref.shape[0]):
                st_ref[slot] = jnp.zeros(st_ref.shape[1:], F32)
            st_ref = st_ref.at[0]
        for d in range(2):
            for h in range(hg):
                st_ref[d, h] = st_s[d * hg + h]

    for h in range(hg):
        sl = slice(h * DN_DK, (h + 1) * DN_DK)
        o = of_s[:, sl] + ob_s[:, sl]
        og_ref[:, sl] = (_rms(o, ng_ref[...]) * _silu(z_ref[:, sl])).astype(BF16)


def _delta_core(main, gb, gt, conv_w, norm_gain, s0, layer_j, seq, hg, n_layers=None, prev_states=None):
    rows = main.shape[0]
    n_seq = rows // seq
    n_chunks = seq // CHUNK
    has_s0 = s0 is not None
    n_groups = DN_HEADS // hg
    width = hg * DN_DK
    aliases = {}

    def col_spec(group):
        return pl.BlockSpec((seq, width), lambda b, g: (b, group * n_groups + g))

    def conv_spec(group):
        return pl.BlockSpec((CONV_K, width), lambda b, g: (0, group * n_groups + g))

    in_specs = [
        col_spec(0), col_spec(1), col_spec(2), col_spec(3),
        pl.BlockSpec((seq, LANES), lambda b, g: (b, 0)),
        pl.BlockSpec((n_chunks, LANES, CHUNK), lambda b, g: (b, 0, 0)),
        conv_spec(0), conv_spec(1), conv_spec(2),
        pl.BlockSpec((1, DN_DK), lambda b, g: (0, 0)),
    ]
    args = [main, main, main, main, gb, gt, conv_w, conv_w, conv_w, norm_gain.reshape(1, DN_DK)]
    out_specs = [pl.BlockSpec((seq, width), lambda b, g: (b, g))]
    out_shape = [jax.ShapeDtypeStruct((rows, DN_KEY), BF16)]
    if has_s0:
        in_specs.append(pl.BlockSpec((None, None, 2, hg, DN_DK, DN_DK), lambda b, g: (b, layer_j, 0, g, 0, 0)))
        args.append(s0)
    else:
        if prev_states is None:
            assert layer_j == 0
            out_specs.append(pl.BlockSpec((None, n_layers, 2, hg, DN_DK, DN_DK), lambda b, g: (b, 0, 0, g, 0, 0)))
        else:
            out_specs.append(pl.BlockSpec((None, None, 2, hg, DN_DK, DN_DK), lambda b, g: (b, layer_j, 0, g, 0, 0)))
        out_shape.append(jax.ShapeDtypeStruct((n_seq, n_layers, 2, DN_HEADS, DN_DK, DN_DK), F32))
        if prev_states is not None:
            aliases = {len(args): 1}
            in_specs.append(pl.BlockSpec(memory_space=pl.ANY))
            args.append(prev_states)
    scratch = [pltpu.VMEM((seq, width), F32) for _ in range(5)]
    scratch += [pltpu.VMEM((seq + 16, width), F32) for _ in range(3)]
    scratch.append(pltpu.VMEM((2 * hg, DN_DK, DN_DK), F32))
    return pl.pallas_call(
        functools.partial(_delta_kernel, seq=seq, hg=hg, has_s0=has_s0, has_prev=bool(aliases)),
        grid=(n_seq, n_groups),
        in_specs=in_specs,
        out_specs=out_specs,
        out_shape=out_shape,
        scratch_shapes=scratch,
        input_output_aliases=aliases,
        compiler_params=_params(2),
        name="delta_core",
    )(*args)


def _post_kernel(*refs, final, ff_chunk):
    x_ref, o_ref, mods_ref, g2_ref, wout_ref, w1_ref, w2_ref = refs[:7]
    fin_ref = refs[7] if final else None
    out_ref = refs[-1]
    x1 = x_ref[...] + mods_ref[2:3, :] * _dot(o_ref[...], wout_ref[...])
    h = (_rms(x1, g2_ref[...]) * (1.0 + mods_ref[4:5, :]) + mods_ref[3:4, :]).astype(BF16)
    acc = jnp.zeros(x1.shape, F32)
    for c in range(D_FF // ff_chunk):
        a = _dot(h, w1_ref[:, c * ff_chunk:(c + 1) * ff_chunk])
        a = jnp.square(jnp.maximum(a, 0.0)).astype(BF16)
        acc = acc + _dot(a, w2_ref[c * ff_chunk:(c + 1) * ff_chunk, :])
    x2 = x1 + mods_ref[5:6, :] * acc
    if final:
        x2 = _rms(x2, fin_ref[...])
    out_ref[...] = x2


def _post(x, o, mods_l, cond_of_tile, gain2, wout_bf, w1_bf, w2_bf, final_gain, tm):
    rows = x.shape[0]
    final = final_gain is not None
    in_specs = [
        pl.BlockSpec((tm, D_MODEL), lambda i: (i, 0)),
        pl.BlockSpec((tm, D_MODEL), lambda i: (i, 0)),
        _mods_spec(cond_of_tile),
        _const_spec((1, D_MODEL)),
        _resident_spec((D_MODEL, D_MODEL)),
        _resident_spec((D_MODEL, D_FF)),
        _resident_spec((D_FF, D_MODEL)),
    ]
    args = [x, o, mods_l, gain2.reshape(1, D_MODEL), wout_bf, w1_bf, w2_bf]
    if final:
        in_specs.append(_const_spec((1, D_MODEL)))
        args.append(final_gain.reshape(1, D_MODEL))
    return pl.pallas_call(
        functools.partial(_post_kernel, final=final, ff_chunk=1024),
        grid=(rows // tm,),
        in_specs=in_specs,
        out_specs=pl.BlockSpec((tm, D_MODEL), lambda i: (i, 0)),
        out_shape=jax.ShapeDtypeStruct((rows, D_MODEL), F32),
        compiler_params=_params(1),
        name="post_mlp",
    )(*args)


def _rope_tables(n_tokens):
    rows = n_tokens // GRID_W
    row = jnp.repeat(jnp.arange(rows), GRID_W).astype(F32)
    col = jnp.tile(jnp.arange(GRID_W), rows).astype(F32)
    inv = ROPE_THETA ** (-jnp.arange(ROPE_FREQS, dtype=F32) / ROPE_FREQS)
    ang_r = row[:, None] * inv
    ang_c = col[:, None] * inv
    cos = jnp.concatenate([jnp.cos(ang_r), jnp.cos(ang_r), jnp.cos(ang_c), jnp.cos(ang_c)], axis=1)
    sin = jnp.concatenate([-jnp.sin(ang_r), jnp.sin(ang_r), -jnp.sin(ang_c), jnp.sin(ang_c)], axis=1)
    return cos, sin


def kernel(x_prompt, x_sample, cache_k, cache_v, state_delta, c, c_ctx, w_mod, b_mod, norm1, norm2,
           attn_w_in, attn_q_gain, attn_k_gain, attn_w_out, dn_w_in, dn_conv, dn_a_log, dn_dt_bias,
           dn_norm, dn_w_out, mlp_w1, mlp_w2, final_norm):
    n_ctx, seq_ctx, _ = x_prompt.shape
    n_lat, seq_lat, _ = x_sample.shape
    past = cache_k.shape[2]
    tm = 256
    tm_post = 512

    cond = jnp.concatenate([c_ctx[None, :], c, jnp.zeros((N_COND - 1 - n_lat, D_MODEL), F32)], axis=0)
    mods = _modulation(cond, w_mod, b_mod).reshape(DEPTH, N_COND, N_MOD, D_MODEL)

    rope_cos, rope_sin = _rope_tables(seq_lat)
    lat_tiles = seq_lat // tm
    cache_k4 = cache_k.reshape(n_lat, cache_k.shape[1], past, KV_WIDTH)
    cache_v4 = cache_v.reshape(n_lat, cache_v.shape[1], past, KV_WIDTH)

    streams = {
        "ctx": dict(x=x_prompt.reshape(n_ctx * seq_ctx, D_MODEL), seq=seq_ctx,
                    cond=lambda rows_per_tile: (lambda i: 0), rope=None),
        "lat": dict(x=x_sample.reshape(n_lat * seq_lat, D_MODEL), seq=seq_lat,
                    cond=lambda rows_per_tile: (lambda i: 1 + i // (seq_lat // rows_per_tile)),
                    rope=lambda i: i % lat_tiles),
    }
    n_attn = (DEPTH + 1) // 2
    n_delta = DEPTH // 2
    new_k = new_v = new_states = None
    assert tm == seq_ctx

    for l in range(DEPTH):
        j = l // 2
        mods_l = mods[l]
        w1_bf = mlp_w1[l].astype(BF16)
        w2_bf = mlp_w2[l].astype(BF16)
        final_gain = final_norm if l == DEPTH - 1 else None
        if l % 2 == 0:
            w_in_bf = attn_w_in[j].astype(BF16)
            w_out_bf = attn_w_out[j].astype(BF16)
        else:
            w_main_bf = dn_w_in[j][:, :DN_MAIN].astype(BF16)
            w_ab_bf = jnp.pad(dn_w_in[j][:, DN_MAIN:], ((0, 0), (0, LANES - 4 * DN_HEADS))).astype(BF16)
            w_out_bf = dn_w_out[j].astype(BF16)
            alog_row = jnp.pad(dn_a_log[j].reshape(1, 2 * DN_HEADS), ((0, 0), (0, LANES - 2 * DN_HEADS)))
            dtb_row = jnp.pad(dn_dt_bias[j].reshape(1, 2 * DN_HEADS), ((0, 0), (0, LANES - 2 * DN_HEADS)))
        for name, st in streams.items():
            x = st["x"]
            seq = st["seq"]
            rows = x.shape[0]
            if l % 2 == 0:
                if name == "ctx":
                    q, new_k, new_v = _attn_proj(x, mods_l, st["cond"](tm), norm1[l], w_in_bf, attn_q_gain[j],
                                                 attn_k_gain[j], rope_cos, rope_sin, st["rope"], tm,
                                                 layered=(j, n_attn, new_k, new_v))
                    o = _context_attention(q, new_k, new_v, j, seq)
                else:
                    q, k, v = _attn_proj(x, mods_l, st["cond"](tm), norm1[l], w_in_bf, attn_q_gain[j], attn_k_gain[j],
                                         rope_cos, rope_sin, st["rope"], tm)
                    o = _latent_attention(q, k, v, cache_k4, cache_v4, j, seq, tm)
            else:
                main, gb, gt = _dn_proj(x, mods_l, st["cond"](tm), norm1[l], w_main_bf, w_ab_bf, alog_row, dtb_row, tm)
                if name == "ctx":
                    o, new_states = _delta_core(main, gb, gt, dn_conv[j], dn_norm[j], None, j, seq, DN_HEADS,
                                                n_layers=n_delta, prev_states=new_states)
                else:
                    (o,) = _delta_core(main, gb, gt, dn_conv[j], dn_norm[j], state_delta, j, seq, DN_HEADS // 2)
            st["x"] = _post(x, o, mods_l, st["cond"](tm_post), norm2[l], w_out_bf, w1_bf, w2_bf, final_gain, tm_post)

    y_prompt = streams["ctx"]["x"].reshape(n_ctx, seq_ctx, D_MODEL)
    y_sample = streams["lat"]["x"].reshape(n_lat, seq_lat, D_MODEL)
    new_cache_k = new_k.reshape(n_ctx, n_attn, seq_ctx, KV_HEADS, HEAD_DIM)
    new_cache_v = new_v.reshape(n_ctx, n_attn, seq_ctx, KV_HEADS, HEAD_DIM)
    return (y_prompt, y_sample, new_cache_k, new_cache_v, new_states)
```

```python
import functools

import jax
import jax.numpy as jnp
from jax import lax
from jax.experimental import pallas as pl
from jax.experimental.pallas import tpu as pltpu

F32 = jnp.float32
BF16 = jnp.bfloat16

D_MODEL = 1024
DEPTH = 4
GRID_W = 64
HEAD_DIM = 128
N_HEADS = 8
KV_HEADS = 2
GROUP = N_HEADS // KV_HEADS
Q_WIDTH = N_HEADS * HEAD_DIM
KV_WIDTH = KV_HEADS * HEAD_DIM
ROPE_THETA = 10000.0
ROPE_FREQS = HEAD_DIM // 4
DN_HEADS = 8
DN_DK = 128
DN_KEY = DN_HEADS * DN_DK
DN_MAIN = 4 * DN_KEY
CONV_K = 5
CHUNK = 64
MAX_UNROLLED_CHUNKS = 4
CHUNKS_PER_STEP = 2
D_FF = 4 * D_MODEL
N_MOD = 6
EPS = 1e-6
N_COND = 8
LANES = 128
VMEM_LIMIT = 52 * 1024 * 1024


def _params(n_grid):
    return pltpu.CompilerParams(dimension_semantics=("arbitrary",) * n_grid,
                                vmem_limit_bytes=VMEM_LIMIT)


def _const_spec(shape):
    nd = len(shape)
    return pl.BlockSpec(shape, lambda *_: (0,) * nd)


def _resident_spec(shape):
    nd = len(shape)
    return pl.BlockSpec(shape, lambda *_: (0,) * nd, pipeline_mode=pl.Buffered(1))


def _dot(a, b):
    return jnp.dot(a, b, preferred_element_type=F32)


def _dot_nt(a, b):
    return lax.dot_general(a, b, (((1,), (1,)), ((), ())), preferred_element_type=F32)


def _silu(x):
    return x * jax.nn.sigmoid(x)


def _rms(x, gain):
    return x * lax.rsqrt(jnp.mean(x * x, axis=-1, keepdims=True) + EPS) * gain


def _mod_kernel(cond_ref, w_ref, b_ref, o_ref):
    s = _silu(cond_ref[...]).astype(BF16)
    o_ref[...] = _dot(s, w_ref[...].astype(BF16)) + b_ref[...]


def _modulation(cond, w_mod, b_mod):
    tn = 1536
    n_out = N_MOD * D_MODEL
    return pl.pallas_call(
        _mod_kernel,
        grid=(DEPTH, n_out // tn),
        in_specs=[
            _const_spec((N_COND, D_MODEL)),
            pl.BlockSpec((None, D_MODEL, tn), lambda l, n: (l, 0, n)),
            pl.BlockSpec((None, 1, tn), lambda l, n: (l, 0, n)),
        ],
        out_specs=pl.BlockSpec((None, N_COND, tn), lambda l, n: (l, 0, n)),
        out_shape=jax.ShapeDtypeStruct((DEPTH, N_COND, n_out), F32),
        compiler_params=_params(2),
        name="modulation",
    )(cond, w_mod, b_mod.reshape(DEPTH, 1, n_out))


def _mods_spec(cond_of_tile):
    return pl.BlockSpec((None, N_MOD, D_MODEL), lambda i: (cond_of_tile(i), 0, 0))


def _attn_proj_kernel(x_ref, mods_ref, gain_ref, w_ref, qg_ref, kg_ref, *rest, use_rope, self_attend):
    q_out, k_out, v_out = rest[-3:]
    x = x_ref[...]
    h = _rms(x, gain_ref[...]) * (1.0 + mods_ref[1:2, :]) + mods_ref[0:1, :]
    proj = _dot(h.astype(BF16), w_ref[...])
    if use_rope:
        cos = rest[0][...]
        sin = rest[1][...]
        lane = lax.broadcasted_iota(jnp.int32, cos.shape, 1)
        low_half = (lane % (2 * ROPE_FREQS)) < ROPE_FREQS

    def prep(xh, gain):
        y = _rms(xh, gain)
        if not use_rope:
            return y
        partner = jnp.where(low_half, pltpu.roll(y, LANES - ROPE_FREQS, 1), pltpu.roll(y, ROPE_FREQS, 1))
        return y * cos + partner * sin

    q_heads = [(prep(proj[:, hh * HEAD_DIM:(hh + 1) * HEAD_DIM], qg_ref[...]) * (HEAD_DIM ** -0.5)).astype(BF16)
               for hh in range(N_HEADS)]
    k_heads = [prep(proj[:, Q_WIDTH + kv * HEAD_DIM:Q_WIDTH + (kv + 1) * HEAD_DIM], kg_ref[...])
               for kv in range(KV_HEADS)]
    v_all = proj[:, Q_WIDTH + KV_WIDTH:]
    if len(k_out.shape) == 3:
        for slot in range(1, k_out.shape[0]):
            k_out[slot] = jnp.zeros(k_out.shape[1:], F32)
            v_out[slot] = jnp.zeros(v_out.shape[1:], F32)
        k_out, v_out = k_out.at[0], v_out.at[0]
    for kv in range(KV_HEADS):
        k_out[:, kv * HEAD_DIM:(kv + 1) * HEAD_DIM] = k_heads[kv]
    v_out[...] = v_all
    if self_attend:
        _attend(lambda hh: q_heads[hh],
                lambda kv: ([k_heads[kv].astype(BF16)], [v_all[:, kv * HEAD_DIM:(kv + 1) * HEAD_DIM].astype(BF16)]),
                q_out)
    else:
        for hh in range(N_HEADS):
            q_out[:, hh * HEAD_DIM:(hh + 1) * HEAD_DIM] = q_heads[hh]


def _attn_proj(x, mods_l, cond_of_tile, gain, w_bf, q_gain, k_gain, rope_cos, rope_sin, rope_of_tile, tm,
               layered=None, self_attend=False):
    rows = x.shape[0]
    use_rope = rope_of_tile is not None
    in_specs = [
        pl.BlockSpec((tm, D_MODEL), lambda i: (i, 0)),
        _mods_spec(cond_of_tile),
        _const_spec((1, D_MODEL)),
        _const_spec((D_MODEL, Q_WIDTH + 2 * KV_WIDTH)),
        _const_spec((1, HEAD_DIM)),
        _const_spec((1, HEAD_DIM)),
    ]
    args = [x, mods_l, gain.reshape(1, D_MODEL), w_bf, q_gain.reshape(1, HEAD_DIM), k_gain.reshape(1, HEAD_DIM)]
    if use_rope:
        in_specs += [pl.BlockSpec((tm, HEAD_DIM), lambda i: (rope_of_tile(i), 0)) for _ in range(2)]
        args += [rope_cos, rope_sin]
    aliases = {}
    if layered is None:
        kv_spec = pl.BlockSpec((tm, KV_WIDTH), lambda i: (i, 0))
        kv_shape = jax.ShapeDtypeStruct((rows, KV_WIDTH), F32)
    else:
        layer_j, n_layers, prev_k, prev_v = layered
        if prev_k is None:
            assert layer_j == 0
            kv_spec = pl.BlockSpec((None, n_layers, tm, KV_WIDTH), lambda i: (i, 0, 0, 0))
        else:
            kv_spec = pl.BlockSpec((None, None, tm, KV_WIDTH), lambda i: (i, layer_j, 0, 0))
        kv_shape = jax.ShapeDtypeStruct((rows // tm, n_layers, tm, KV_WIDTH), F32)
        if prev_k is not None:
            aliases = {len(args): 1, len(args) + 1: 2}
            in_specs += [pl.BlockSpec(memory_space=pl.ANY), pl.BlockSpec(memory_space=pl.ANY)]
            args += [prev_k, prev_v]
    return pl.pallas_call(
        functools.partial(_attn_proj_kernel, use_rope=use_rope, self_attend=self_attend),
        grid=(rows // tm,),
        in_specs=in_specs,
        out_specs=[pl.BlockSpec((tm, Q_WIDTH), lambda i: (i, 0)), kv_spec, kv_spec],
        out_shape=[jax.ShapeDtypeStruct((rows, Q_WIDTH), BF16), kv_shape, kv_shape],
        input_output_aliases=aliases,
        compiler_params=_params(1),
        name="attn_proj",
    )(*args)


def _attend(q_head, kv_pieces, o_ref):
    for kv in range(KV_HEADS):
        ks, vs = kv_pieces(kv)
        heads = [kv * GROUP + g for g in range(GROUP)]
        scores = [[_dot_nt(q_head(hh), k) for k in ks] for hh in heads]
        probs, denoms = [], []
        for ss in scores:
            m = functools.reduce(jnp.maximum, [jnp.max(s, axis=-1, keepdims=True) for s in ss])
            ps = [jnp.exp(s - m) for s in ss]
            denoms.append(functools.reduce(jnp.add, [jnp.sum(p, axis=-1, keepdims=True) for p in ps]))
            probs.append([p.astype(BF16) for p in ps])
        outs = [functools.reduce(jnp.add, [_dot(p, v) for p, v in zip(ps, vs)]) for ps in probs]
        for hh, o, denom in zip(heads, outs, denoms):
            o_ref[:, hh * HEAD_DIM:(hh + 1) * HEAD_DIM] = (o / denom).astype(BF16)


def _attn_kernel(*refs, n_pieces):
    q_ref = refs[0]
    kv_refs = refs[1:1 + 2 * n_pieces]
    o_ref = refs[1 + 2 * n_pieces]
    q = q_ref[...]

    def kv_pieces(kv):
        sl = slice(kv * HEAD_DIM, (kv + 1) * HEAD_DIM)
        return ([kv_refs[2 * p][:, sl].astype(BF16) for p in range(n_pieces)],
                [kv_refs[2 * p + 1][:, sl].astype(BF16) for p in range(n_pieces)])

    _attend(lambda hh: q[:, hh * HEAD_DIM:(hh + 1) * HEAD_DIM], kv_pieces, o_ref)


def _latent_attention(q, k, v, cache_k4, cache_v4, layer_j, seq, tq):
    rows = q.shape[0]
    past = cache_k4.shape[2]
    nq = seq // tq
    cache_spec = pl.BlockSpec((None, None, past, KV_WIDTH), lambda b, i: (b, layer_j, 0, 0))
    return pl.pallas_call(
        functools.partial(_attn_kernel, n_pieces=2),
        grid=(rows // seq, nq),
        in_specs=[
            pl.BlockSpec((tq, Q_WIDTH), lambda b, i: (b * nq + i, 0)),
            cache_spec,
            cache_spec,
            pl.BlockSpec((seq, KV_WIDTH), lambda b, i: (b, 0)),
            pl.BlockSpec((seq, KV_WIDTH), lambda b, i: (b, 0)),
        ],
        out_specs=pl.BlockSpec((tq, Q_WIDTH), lambda b, i: (b * nq + i, 0)),
        out_shape=jax.ShapeDtypeStruct((rows, Q_WIDTH), BF16),
        compiler_params=_params(2),
        name="latent_attention",
    )(q, cache_k4, cache_v4, k, v)


def _dn_proj_kernel(x_ref, mods_ref, gain_ref, w_ref, wab_ref, alog_ref, dtb_ref, main_out, gb_out, gt_out):
    x = x_ref[...]
    h = (_rms(x, gain_ref[...]) * (1.0 + mods_ref[1:2, :]) + mods_ref[0:1, :]).astype(BF16)
    ab = _dot(h, wab_ref[...])
    main_out[...] = _dot(h, w_ref[...])
    lane = lax.broadcasted_iota(jnp.int32, ab.shape, 1)
    g = -jnp.exp(alog_ref[...]) * jax.nn.softplus(ab + dtb_ref[...])
    tm = ab.shape[0]
    r = lax.broadcasted_iota(jnp.int32, (tm, tm), 0)
    c = lax.broadcasted_iota(jnp.int32, (tm, tm), 1)
    same_chunk = (r // CHUNK) == (c // CHUNK)
    m_fwd = jnp.where(same_chunk, jnp.where(c <= r, 1.0, 0.0), 0.0).astype(BF16)
    m_bwd = jnp.where(same_chunk, jnp.where(c >= r, 1.0, 0.0), 0.0).astype(BF16)
    g1 = g.astype(BF16)
    rem = g - g1.astype(F32)
    g2 = rem.astype(BF16)
    g3 = (rem - g2.astype(F32)).astype(BF16)
    cum_f = _dot(m_fwd, g1) + (_dot(m_fwd, g2) + _dot(m_fwd, g3))
    cum_b = _dot(m_bwd, g1) + (_dot(m_bwd, g2) + _dot(m_bwd, g3))
    total = cum_f + cum_b - g
    out = jnp.where(lane < DN_HEADS, cum_f, jnp.where(lane < 2 * DN_HEADS, cum_b, jax.nn.sigmoid(ab)))
    table = jnp.where(lane < 4 * DN_HEADS, out, pltpu.roll(total, 4 * DN_HEADS, 1))
    gb_out[...] = table
    for ch in range(tm // CHUNK):
        gt_out[ch] = jnp.transpose(table[ch * CHUNK:(ch + 1) * CHUNK, :])


def _dn_proj(x, mods_l, cond_of_tile, gain, w_main_bf, w_ab_bf, alog_row, dtb_row, tm):
    rows = x.shape[0]
    return pl.pallas_call(
        _dn_proj_kernel,
        grid=(rows // tm,),
        in_specs=[
            pl.BlockSpec((tm, D_MODEL), lambda i: (i, 0)),
            _mods_spec(cond_of_tile),
            _const_spec((1, D_MODEL)),
            _const_spec((D_MODEL, DN_MAIN)),
            _const_spec((D_MODEL, LANES)),
            _const_spec((1, LANES)),
            _const_spec((1, LANES)),
        ],
        out_specs=[
            pl.BlockSpec((tm, DN_MAIN), lambda i: (i, 0)),
            pl.BlockSpec((tm, LANES), lambda i: (i, 0)),
            pl.BlockSpec((tm // CHUNK, LANES, CHUNK), lambda i: (i, 0, 0)),
        ],
        out_shape=[
            jax.ShapeDtypeStruct((rows, DN_MAIN), F32),
            jax.ShapeDtypeStruct((rows, LANES), F32),
            jax.ShapeDtypeStruct((rows // CHUNK, LANES, CHUNK), F32),
        ],
        compiler_params=_params(1),
        name="dn_proj",
    )(x, mods_l, gain.reshape(1, D_MODEL), w_main_bf, w_ab_bf, alog_row, dtb_row)


def _split_bf16(a):
    hi = a.astype(BF16)
    lo = (a - hi.astype(F32)).astype(BF16)
    return hi, lo


def _unit_tri_inverse_pair(low, eye, block_diag_mask):
    c = CHUNK

    def block_diag(x_bf):
        return jnp.where(block_diag_mask, jnp.concatenate([x_bf, x_bf], axis=0), jnp.zeros((), BF16))

    p = eye - low
    m = low
    steps = CHUNK.bit_length() - 2
    for it in range(steps + 1):
        need_square = it < steps
        need_pm = it > 0
        mh, ml = _split_bf16(m)
        lhs_a, lhs_b = [], []
        if need_square:
            lhs_a += [mh, ml]
            lhs_b += [mh]
        if need_pm:
            ph, pl_ = _split_bf16(p)
            lhs_a += [ph, pl_]
            lhs_b += [ph]
        a = _dot(jnp.concatenate(lhs_a, axis=0), block_diag(mh))
        b = _dot(jnp.concatenate(lhs_b, axis=0), block_diag(ml))
        off_a = off_b = 0
        if need_square:
            m = a[0:c] + a[c:2 * c] + b[0:c]
            off_a, off_b = 2 * c, c
        if need_pm:
            p = p + (a[off_a:off_a + c] + a[off_a + c:off_a + 2 * c] + b[off_b:off_b + c])
        yield
    return p


def _delta_kernel(*refs, seq, hg, has_s0, has_prev):
    (q_ref, k_ref, v_ref, z_ref, gb_ref, gt_ref, cq_ref, ck_ref, cv_ref, ng_ref) = refs[:10]
    pos = 10
    s0_ref = None
    if has_s0:
        s0_ref = refs[pos]
        pos += 1
    if has_prev:
        pos += 1
    og_ref = refs[pos]
    pos += 1
    st_ref = None
    if not has_s0:
        st_ref = refs[pos]
        pos += 1
    q_s, k_s, v_s, of_s, ob_s, pad_q, pad_k, pad_v, st_s = refs[pos:]

    head0 = pl.program_id(1) * hg
    n_chunks = seq // CHUNK
    halo = 8
    width = hg * DN_DK
    unrolled = n_chunks <= MAX_UNROLLED_CHUNKS

    for pad, x_ref in ((pad_q, q_ref), (pad_k, k_ref), (pad_v, v_ref)):
        pad[0:halo, :] = jnp.zeros((halo, width), F32)
        pad[halo + seq:2 * halo + seq, :] = jnp.zeros((halo, width), F32)
        pad[halo:halo + seq, :] = x_ref[...]

    def prep_chunk(c):
        r = c * CHUNK
        base = halo - CONV_K // 2 + r

        def conv_silu(pad, cw_ref):
            win = pad[r:r + CHUNK + 2 * halo, :]
            n_win = CHUNK + 2 * halo
            acc = None
            for j in range(CONV_K):
                shift = (CONV_K // 2 - j) % n_win
                tap = win if shift == 0 else pltpu.roll(win, shift, 0)
                term = tap[halo:halo + CHUNK, :] * cw_ref[j:j + 1, :]
                acc = term if acc is None else acc + term
            return _silu(acc)

        def l2n_heads(x, scale, dst):
            for h in range(hg):
                sl = slice(h * DN_DK, (h + 1) * DN_DK)
                xh = x[:, sl]
                dst[r:r + CHUNK, sl] = xh * (lax.rsqrt(jnp.sum(xh * xh, axis=-1, keepdims=True) + EPS) * scale)

        l2n_heads(conv_silu(pad_q, cq_ref), DN_DK ** -0.5, q_s)
        l2n_heads(conv_silu(pad_k, ck_ref), 1.0, k_s)
        v_s[r:r + CHUNK, :] = conv_silu(pad_v, cv_ref)

    order = []
    for c in range(n_chunks):
        for cc in (c, n_chunks - 1 - c):
            if cc not in order:
                order.append(cc)
    for c in order:
        prep_chunk(c)

    for d in range(2):
        for h in range(hg):
            if has_s0:
                st_s[d * hg + h] = s0_ref[d, h]
            else:
                st_s[d * hg + h] = jnp.zeros((DN_DK, DN_DK), F32)

    pair_w = 2 * CHUNK
    row_p = lax.broadcasted_iota(jnp.int32, (CHUNK, pair_w), 0)
    lane_p = lax.broadcasted_iota(jnp.int32, (CHUNK, pair_w), 1)
    col_p = lane_p & (CHUNK - 1)
    in_b = lane_p >= CHUNK
    eye = (row_p == col_p).astype(F32)
    bd_r = lax.broadcasted_iota(jnp.int32, (pair_w, pair_w), 0)
    bd_c = lax.broadcasted_iota(jnp.int32, (pair_w, pair_w), 1)
    block_diag_mask = (bd_r >= CHUNK) == (bd_c >= CHUNK)
    lane = lax.broadcasted_iota(jnp.int32, (CHUNK, LANES), 1)
    zeros_k = jnp.zeros((CHUNK, DN_DK), BF16)
    kq_cache = {}

    def pair_step(n, ha, direction):
        hb = ha + 1
        r0 = n * CHUNK if unrolled else pl.multiple_of(n * CHUNK, CHUNK)
        sls = [slice(h * DN_DK, (h + 1) * DN_DK) for h in (ha, hb)]
        qs = [q_s[pl.ds(r0, CHUNK), sl] for sl in sls]
        ks = [k_s[pl.ds(r0, CHUNK), sl] for sl in sls]
        vs = [v_s[pl.ds(r0, CHUNK), sl].astype(BF16) for sl in sls]
        ks_bf = [k.astype(BF16) for k in ks]
        gbc = gb_ref[pl.ds(r0, CHUNK), :]
        col_a = direction * DN_HEADS + head0 + ha

        def col_vec(c):
            return jnp.sum(jnp.where(lane == c, gbc, 0.0), axis=1, keepdims=True)

        def row_vec(r):
            both = gt_ref[n, pl.ds(r, 2), :]
            return jnp.concatenate([both[0:1], both[1:2]], axis=1)

        gcs = [col_vec(col_a), col_vec(col_a + 1)]
        gc_col = jnp.where(in_b, gcs[1], gcs[0])
        b_col = jnp.where(in_b, col_vec(col_a + 1 + 2 * DN_HEADS), col_vec(col_a + 2 * DN_HEADS))
        gc_row = row_vec(col_a)
        b_row = row_vec(col_a + 2 * DN_HEADS)
        tot = gt_ref[n, pl.ds(col_a + 4 * DN_HEADS, 2), :]
        g_tots = [tot[0:1, 0:1], tot[1:2, 0:1]]
        if direction == 0:
            incl = col_p <= row_p
            strict = col_p < row_p
        else:
            incl = col_p >= row_p
            strict = col_p > row_p
        decay = jnp.exp(jnp.where(incl, gc_col - gc_row, 0.0))
        kq = kq_cache.get((n, ha)) if unrolled else None
        if kq is None:
            kq = (_dot_nt(jnp.concatenate([ks_bf[0], qs[0].astype(BF16)], axis=0),
                          jnp.concatenate([ks_bf[0], zeros_k], axis=0))
                  + _dot_nt(jnp.concatenate([ks_bf[1], qs[1].astype(BF16)], axis=0),
                            jnp.concatenate([zeros_k, ks_bf[1]], axis=0)))
            if unrolled:
                kq_cache[(n, ha)] = kq
        yield
        low = jnp.where(strict, kq[0:CHUNK] * (decay * b_col), 0.0)
        qk = jnp.where(incl, kq[CHUNK:2 * CHUNK] * decay, 0.0).astype(BF16)
        tinv = yield from _unit_tri_inverse_pair(low, eye, block_diag_mask)
        t_u = (tinv * b_row).astype(BF16)
        t_w = (tinv * (b_row * jnp.exp(gc_row))).astype(BF16)
        pick = [lambda x: jnp.concatenate([x, zeros_k], axis=0), lambda x: jnp.concatenate([zeros_k, x], axis=0)]
        us = [_dot(t_u, pick[i](vs[i])) for i in range(2)]
        ws = [_dot(t_w, pick[i](ks_bf[i])) for i in range(2)]
        qgs = [(qs[i] * jnp.exp(gcs[i])).astype(BF16) for i in range(2)]
        kd = jnp.concatenate([ks[i] * jnp.exp(g_tots[i] - gcs[i]) for i in range(2)], axis=0)
        kd_t = jnp.transpose(kd).astype(BF16)
        yield
        return dict(us=us, ws=[w.astype(BF16) for w in ws], qgs=qgs, lhs_od=jnp.concatenate([qk, kd_t], axis=0),
                    decay_tot=[jnp.exp(g) for g in g_tots], pick=pick, r0=r0, sls=sls,
                    dst=of_s if direction == 0 else ob_s)

    def pair_scan(a, states):
        if states is None:
            yield
            v_new = [a["us"][i].astype(BF16) for i in range(2)]
        else:
            ws_qs = [_dot(jnp.concatenate([a["ws"][i], a["qgs"][i]], axis=0), states[i].astype(BF16))
                     for i in range(2)]
            yield
            v_new = [(a["us"][i] - ws_qs[i][0:CHUNK]).astype(BF16) for i in range(2)]
        ods = [_dot(a["lhs_od"], a["pick"][i](v_new[i])) for i in range(2)]
        yield
        if states is None:
            return [ods[i][CHUNK:CHUNK + DN_DK] for i in range(2)], [ods[i][0:CHUNK] for i in range(2)]
        new_states = [states[i] * a["decay_tot"][i] + ods[i][CHUNK:CHUNK + DN_DK] for i in range(2)]
        outs = [ws_qs[i][CHUNK:2 * CHUNK] + ods[i][0:CHUNK] for i in range(2)]
        return new_states, outs

    def round_robin(gens):
        results = [None] * len(gens)
        while any(r is None for r in results):
            for gi, gen in enumerate(gens):
                if results[gi] is None:
                    try:
                        next(gen)
                    except StopIteration as done:
                        results[gi] = done.value
        return results

    lanes_of = [(ha, d) for ha in range(0, hg, 2) for d in range(2)]

    def body(it, carry):
        def chunk_of(s, d):
            c = it * sub + s
            return c if d == 0 else n_chunks - 1 - c

        preps = round_robin([pair_step(chunk_of(s, d), ha, d) for s in range(sub) for ha, d in lanes_of])
        if unrolled and not has_s0 and it == 0:
            states = [None] * len(lanes_of)
        else:
            states = [[st_s[d * hg + ha], st_s[d * hg + ha + 1]] for ha, d in lanes_of]
        stores = []
        for s in range(sub):
            mine = preps[s * len(lanes_of):(s + 1) * len(lanes_of)]
            scanned = round_robin([pair_scan(a, st) for a, st in zip(mine, states)])
            states = [new for new, _ in scanned]
            stores += [(a, outs) for a, (_, outs) in zip(mine, scanned)]
        for (ha, d), st in zip(lanes_of, states):
            st_s[d * hg + ha] = st[0]
            st_s[d * hg + ha + 1] = st[1]
        for a, outs in stores:
            for i in range(2):
                a["dst"][pl.ds(a["r0"], CHUNK), a["sls"][i]] = outs[i]
        return carry

    if unrolled:
        sub = 1
        for n in range(n_chunks):
            body(n, 0)
    else:
        sub = CHUNKS_PER_STEP
        lax.fori_loop(0, n_chunks // sub, body, 0)
    if st_ref is not None:
        if len(st_ref.shape) == 5:
            for slot in range(1, st_ref.shape[0]):
                st_ref[slot] = jnp.zeros(st_ref.shape[1:], F32)
            st_ref = st_ref.at[0]
        for d in range(2):
            for h in range(hg):
                st_ref[d, h] = st_s[d * hg + h]

    for h in range(hg):
        sl = slice(h * DN_DK, (h + 1) * DN_DK)
        o = of_s[:, sl] + ob_s[:, sl]
        og_ref[:, sl] = (_rms(o, ng_ref[...]) * _silu(z_ref[:, sl])).astype(BF16)


def _delta_core(main, gb, gt, conv_w, norm_gain, s0, layer_j, seq, hg, n_layers=None, prev_states=None):
    rows = main.shape[0]
    n_seq = rows // seq
    n_chunks = seq // CHUNK
    has_s0 = s0 is not None
    n_groups = DN_HEADS // hg
    width = hg * DN_DK
    aliases = {}

    def col_spec(group):
        return pl.BlockSpec((seq, width), lambda b, g: (b, group * n_groups + g))

    def conv_spec(group):
        return pl.BlockSpec((CONV_K, width), lambda b, g: (0, group * n_groups + g))

    in_specs = [
        col_spec(0), col_spec(1), col_spec(2), col_spec(3),
        pl.BlockSpec((seq, LANES), lambda b, g: (b, 0)),
        pl.BlockSpec((n_chunks, LANES, CHUNK), lambda b, g: (b, 0, 0)),
        conv_spec(0), conv_spec(1), conv_spec(2),
        pl.BlockSpec((1, DN_DK), lambda b, g: (0, 0)),
    ]
    args = [main, main, main, main, gb, gt, conv_w, conv_w, conv_w, norm_gain.reshape(1, DN_DK)]
    out_specs = [pl.BlockSpec((seq, width), lambda b, g: (b, g))]
    out_shape = [jax.ShapeDtypeStruct((rows, DN_KEY), BF16)]
    if has_s0:
        in_specs.append(pl.BlockSpec((None, None, 2, hg, DN_DK, DN_DK), lambda b, g: (b, layer_j, 0, g, 0, 0)))
        args.append(s0)
    else:
        if prev_states is None:
            assert layer_j == 0
            out_specs.append(pl.BlockSpec((None, n_layers, 2, hg, DN_DK, DN_DK), lambda b, g: (b, 0, 0, g, 0, 0)))
        else:
            out_specs.append(pl.BlockSpec((None, None, 2, hg, DN_DK, DN_DK), lambda b, g: (b, layer_j, 0, g, 0, 0)))
        out_shape.append(jax.ShapeDtypeStruct((n_seq, n_layers, 2, DN_HEADS, DN_DK, DN_DK), F32))
        if prev_states is not None:
            aliases = {len(args): 1}
            in_specs.append(pl.BlockSpec(memory_space=pl.ANY))
            args.append(prev_states)
    scratch = [pltpu.VMEM((seq, width), F32) for _ in range(5)]
    scratch += [pltpu.VMEM((seq + 16, width), F32) for _ in range(3)]
    scratch.append(pltpu.VMEM((2 * hg, DN_DK, DN_DK), F32))
    return pl.pallas_call(
        functools.partial(_delta_kernel, seq=seq, hg=hg, has_s0=has_s0, has_prev=bool(aliases)),
        grid=(n_seq, n_groups),
        in_specs=in_specs,
        out_specs=out_specs,
        out_shape=out_shape,
        scratch_shapes=scratch,
        input_output_aliases=aliases,
        compiler_params=_params(2),
        name="delta_core",
    )(*args)


def _post_kernel(*refs, final, ff_chunk):
    x_ref, o_ref, mods_ref, g2_ref, wout_ref, w1_ref, w2_ref = refs[:7]
    fin_ref = refs[7] if final else None
    out_ref = refs[-1]
    x1 = x_ref[...] + mods_ref[2:3, :] * _dot(o_ref[...], wout_ref[...])
    h = (_rms(x1, g2_ref[...]) * (1.0 + mods_ref[4:5, :]) + mods_ref[3:4, :]).astype(BF16)
    acc = jnp.zeros(x1.shape, F32)
    for c in range(D_FF // ff_chunk):
        a = _dot(h, w1_ref[:, c * ff_chunk:(c + 1) * ff_chunk])
        a = jnp.square(jnp.maximum(a, 0.0)).astype(BF16)
        acc = acc + _dot(a, w2_ref[c * ff_chunk:(c + 1) * ff_chunk, :])
    x2 = x1 + mods_ref[5:6, :] * acc
    if final:
        x2 = _rms(x2, fin_ref[...])
    out_ref[...] = x2


def _post(x, o, mods_l, cond_of_tile, gain2, wout_bf, w1_bf, w2_bf, final_gain, tm):
    rows = x.shape[0]
    final = final_gain is not None
    in_specs = [
        pl.BlockSpec((tm, D_MODEL), lambda i: (i, 0)),
        pl.BlockSpec((tm, D_MODEL), lambda i: (i, 0)),
        _mods_spec(cond_of_tile),
        _const_spec((1, D_MODEL)),
        _resident_spec((D_MODEL, D_MODEL)),
        _resident_spec((D_MODEL, D_FF)),
        _resident_spec((D_FF, D_MODEL)),
    ]
    args = [x, o, mods_l, gain2.reshape(1, D_MODEL), wout_bf, w1_bf, w2_bf]
    if final:
        in_specs.append(_const_spec((1, D_MODEL)))
        args.append(final_gain.reshape(1, D_MODEL))
    return pl.pallas_call(
        functools.partial(_post_kernel, final=final, ff_chunk=1024),
        grid=(rows // tm,),
        in_specs=in_specs,
        out_specs=pl.BlockSpec((tm, D_MODEL), lambda i: (i, 0)),
        out_shape=jax.ShapeDtypeStruct((rows, D_MODEL), F32),
        compiler_params=_params(1),
        name="post_mlp",
    )(*args)


def _rope_tables(n_tokens):
    rows = n_tokens // GRID_W
    row = jnp.repeat(jnp.arange(rows), GRID_W).astype(F32)
    col = jnp.tile(jnp.arange(GRID_W), rows).astype(F32)
    inv = ROPE_THETA ** (-jnp.arange(ROPE_FREQS, dtype=F32) / ROPE_FREQS)
    ang_r = row[:, None] * inv
    ang_c = col[:, None] * inv
    cos = jnp.concatenate([jnp.cos(ang_r), jnp.cos(ang_r), jnp.cos(ang_c), jnp.cos(ang_c)], axis=1)
    sin = jnp.concatenate([-jnp.sin(ang_r), jnp.sin(ang_r), -jnp.sin(ang_c), jnp.sin(ang_c)], axis=1)
    return cos, sin


def kernel(x_prompt, x_sample, cache_k, cache_v, state_delta, c, c_ctx, w_mod, b_mod, norm1, norm2,
           attn_w_in, attn_q_gain, attn_k_gain, attn_w_out, dn_w_in, dn_conv, dn_a_log, dn_dt_bias,
           dn_norm, dn_w_out, mlp_w1, mlp_w2, final_norm):
    n_ctx, seq_ctx, _ = x_prompt.shape
    n_lat, seq_lat, _ = x_sample.shape
    past = cache_k.shape[2]
    tm = 256
    tm_post = 512

    cond = jnp.concatenate([c_ctx[None, :], c, jnp.zeros((N_COND - 1 - n_lat, D_MODEL), F32)], axis=0)
    mods = _modulation(cond, w_mod, b_mod).reshape(DEPTH, N_COND, N_MOD, D_MODEL)

    rope_cos, rope_sin = _rope_tables(seq_lat)
    lat_tiles = seq_lat // tm
    cache_k4 = cache_k.reshape(n_lat, cache_k.shape[1], past, KV_WIDTH)
    cache_v4 = cache_v.reshape(n_lat, cache_v.shape[1], past, KV_WIDTH)

    streams = {
        "ctx": dict(x=x_prompt.reshape(n_ctx * seq_ctx, D_MODEL), seq=seq_ctx,
                    cond=lambda rows_per_tile: (lambda i: 0), rope=None),
        "lat": dict(x=x_sample.reshape(n_lat * seq_lat, D_MODEL), seq=seq_lat,
                    cond=lambda rows_per_tile: (lambda i: 1 + i // (seq_lat // rows_per_tile)),
                    rope=lambda i: i % lat_tiles),
    }
    n_attn = (DEPTH + 1) // 2
    n_delta = DEPTH // 2
    new_k = new_v = new_states = None
    assert tm == seq_ctx

    for l in range(DEPTH):
        j = l // 2
        mods_l = mods[l]
        w1_bf = mlp_w1[l].astype(BF16)
        w2_bf = mlp_w2[l].astype(BF16)
        final_gain = final_norm if l == DEPTH - 1 else None
        if l % 2 == 0:
            w_in_bf = attn_w_in[j].astype(BF16)
            w_out_bf = attn_w_out[j].astype(BF16)
        else:
            w_main_bf = dn_w_in[j][:, :DN_MAIN].astype(BF16)
            w_ab_bf = jnp.pad(dn_w_in[j][:, DN_MAIN:], ((0, 0), (0, LANES - 4 * DN_HEADS))).astype(BF16)
            w_out_bf = dn_w_out[j].astype(BF16)
            alog_row = jnp.pad(dn_a_log[j].reshape(1, 2 * DN_HEADS), ((0, 0), (0, LANES - 2 * DN_HEADS)))
            dtb_row = jnp.pad(dn_dt_bias[j].reshape(1, 2 * DN_HEADS), ((0, 0), (0, LANES - 2 * DN_HEADS)))
        for name, st in streams.items():
            x = st["x"]
            seq = st["seq"]
            rows = x.shape[0]
            if l % 2 == 0:
                if name == "ctx":
                    o, new_k, new_v = _attn_proj(x, mods_l, st["cond"](tm), norm1[l], w_in_bf, attn_q_gain[j],
                                                 attn_k_gain[j], rope_cos, rope_sin, st["rope"], tm,
                                                 layered=(j, n_attn, new_k, new_v), self_attend=True)
                else:
                    q, k, v = _attn_proj(x, mods_l, st["cond"](tm), norm1[l], w_in_bf, attn_q_gain[j], attn_k_gain[j],
                                         rope_cos, rope_sin, st["rope"], tm)
                    o = _latent_attention(q, k, v, cache_k4, cache_v4, j, seq, tm)
            else:
                main, gb, gt = _dn_proj(x, mods_l, st["cond"](tm), norm1[l], w_main_bf, w_ab_bf, alog_row, dtb_row, tm)
                if name == "ctx":
                    o, new_states = _delta_core(main, gb, gt, dn_conv[j], dn_norm[j], None, j, seq, DN_HEADS,
                                                n_layers=n_delta, prev_states=new_states)
                else:
                    (o,) = _delta_core(main, gb, gt, dn_conv[j], dn_norm[j], state_delta, j, seq, DN_HEADS // 2)
            st["x"] = _post(x, o, mods_l, st["cond"](tm_post), norm2[l], w_out_bf, w1_bf, w2_bf, final_gain, tm_post)

    y_prompt = streams["ctx"]["x"].reshape(n_ctx, seq_ctx, D_MODEL)
    y_sample = streams["lat"]["x"].reshape(n_lat, seq_lat, D_MODEL)
    new_cache_k = new_k.reshape(n_ctx, n_attn, seq_ctx, KV_HEADS, HEAD_DIM)
    new_cache_v = new_v.reshape(n_ctx, n_attn, seq_ctx, KV_HEADS, HEAD_DIM)
    return (y_prompt, y_sample, new_cache_k, new_cache_v, new_states)
```

```python
import functools

import jax
import jax.numpy as jnp
from jax import lax
from jax.experimental import pallas as pl
from jax.experimental.pallas import tpu as pltpu

F32 = jnp.float32
BF16 = jnp.bfloat16

D_MODEL = 1024
DEPTH = 4
GRID_W = 64
HEAD_DIM = 128
N_HEADS = 8
KV_HEADS = 2
GROUP = N_HEADS // KV_HEADS
Q_WIDTH = N_HEADS * HEAD_DIM
KV_WIDTH = KV_HEADS * HEAD_DIM
ROPE_THETA = 10000.0
ROPE_FREQS = HEAD_DIM // 4
DN_HEADS = 8
DN_DK = 128
DN_KEY = DN_HEADS * DN_DK
DN_MAIN = 4 * DN_KEY
CONV_K = 5
CHUNK = 64
MAX_UNROLLED_CHUNKS = 4
CHUNKS_PER_STEP = 2
D_FF = 4 * D_MODEL
N_MOD = 6
EPS = 1e-6
N_COND = 8
LANES = 128
VMEM_LIMIT = 52 * 1024 * 1024


def _params(n_grid):
    return pltpu.CompilerParams(dimension_semantics=("arbitrary",) * n_grid,
                                vmem_limit_bytes=VMEM_LIMIT)


def _const_spec(shape):
    nd = len(shape)
    return pl.BlockSpec(shape, lambda *_: (0,) * nd)


def _resident_spec(shape):
    nd = len(shape)
    return pl.BlockSpec(shape, lambda *_: (0,) * nd, pipeline_mode=pl.Buffered(1))


def _dot(a, b):
    return jnp.dot(a, b, preferred_element_type=F32)


def _dot_nt(a, b):
    return lax.dot_general(a, b, (((1,), (1,)), ((), ())), preferred_element_type=F32)


def _silu(x):
    return x * jax.nn.sigmoid(x)


def _rms(x, gain):
    return x * lax.rsqrt(jnp.mean(x * x, axis=-1, keepdims=True) + EPS) * gain


def _mod_kernel(cond_ref, w_ref, b_ref, o_ref):
    s = _silu(cond_ref[...]).astype(BF16)
    o_ref[...] = _dot(s, w_ref[...].astype(BF16)) + b_ref[...]


def _modulation(cond, w_mod, b_mod):
    tn = 1536
    n_out = N_MOD * D_MODEL
    return pl.pallas_call(
        _mod_kernel,
        grid=(DEPTH, n_out // tn),
        in_specs=[
            _const_spec((N_COND, D_MODEL)),
            pl.BlockSpec((None, D_MODEL, tn), lambda l, n: (l, 0, n)),
            pl.BlockSpec((None, 1, tn), lambda l, n: (l, 0, n)),
        ],
        out_specs=pl.BlockSpec((None, N_COND, tn), lambda l, n: (l, 0, n)),
        out_shape=jax.ShapeDtypeStruct((DEPTH, N_COND, n_out), F32),
        compiler_params=_params(2),
        name="modulation",
    )(cond, w_mod, b_mod.reshape(DEPTH, 1, n_out))


def _mods_spec(cond_of_tile):
    return pl.BlockSpec((None, N_MOD, D_MODEL), lambda i: (cond_of_tile(i), 0, 0))


def _attn_proj_kernel(x_ref, mods_ref, gain_ref, w_ref, qg_ref, kg_ref, *rest, use_rope, self_attend):
    q_out, k_out, v_out = rest[-3:]
    x = x_ref[...]
    h = _rms(x, gain_ref[...]) * (1.0 + mods_ref[1:2, :]) + mods_ref[0:1, :]
    proj = _dot(h.astype(BF16), w_ref[...])
    if use_rope:
        cos = rest[0][...]
        sin = rest[1][...]
        lane = lax.broadcasted_iota(jnp.int32, cos.shape, 1)
        low_half = (lane % (2 * ROPE_FREQS)) < ROPE_FREQS

    def prep(xh, gain):
        y = _rms(xh, gain)
        if not use_rope:
            return y
        partner = jnp.where(low_half, pltpu.roll(y, LANES - ROPE_FREQS, 1), pltpu.roll(y, ROPE_FREQS, 1))
        return y * cos + partner * sin

    q_heads = [(prep(proj[:, hh * HEAD_DIM:(hh + 1) * HEAD_DIM], qg_ref[...]) * (HEAD_DIM ** -0.5)).astype(BF16)
               for hh in range(N_HEADS)]
    k_heads = [prep(proj[:, Q_WIDTH + kv * HEAD_DIM:Q_WIDTH + (kv + 1) * HEAD_DIM], kg_ref[...])
               for kv in range(KV_HEADS)]
    v_all = proj[:, Q_WIDTH + KV_WIDTH:]
    if len(k_out.shape) == 3:
        for slot in range(1, k_out.shape[0]):
            k_out[slot] = jnp.zeros(k_out.shape[1:], F32)
            v_out[slot] = jnp.zeros(v_out.shape[1:], F32)
        k_out, v_out = k_out.at[0], v_out.at[0]
    for kv in range(KV_HEADS):
        k_out[:, kv * HEAD_DIM:(kv + 1) * HEAD_DIM] = k_heads[kv]
    v_out[...] = v_all
    if self_attend:
        _attend(lambda hh: q_heads[hh],
                lambda kv: ([k_heads[kv].astype(BF16)], [v_all[:, kv * HEAD_DIM:(kv + 1) * HEAD_DIM].astype(BF16)]),
                q_out)
    else:
        for hh in range(N_HEADS):
            q_out[:, hh * HEAD_DIM:(hh + 1) * HEAD_DIM] = q_heads[hh]


def _attn_proj(x, mods_l, cond_of_tile, gain, w_bf, q_gain, k_gain, rope_cos, rope_sin, rope_of_tile, tm,
               layered=None, self_attend=False):
    rows = x.shape[0]
    use_rope = rope_of_tile is not None
    in_specs = [
        pl.BlockSpec((tm, D_MODEL), lambda i: (i, 0)),
        _mods_spec(cond_of_tile),
        _const_spec((1, D_MODEL)),
        _const_spec((D_MODEL, Q_WIDTH + 2 * KV_WIDTH)),
        _const_spec((1, HEAD_DIM)),
        _const_spec((1, HEAD_DIM)),
    ]
    args = [x, mods_l, gain.reshape(1, D_MODEL), w_bf, q_gain.reshape(1, HEAD_DIM), k_gain.reshape(1, HEAD_DIM)]
    if use_rope:
        in_specs += [pl.BlockSpec((tm, HEAD_DIM), lambda i: (rope_of_tile(i), 0)) for _ in range(2)]
        args += [rope_cos, rope_sin]
    aliases = {}
    if layered is None:
        kv_spec = pl.BlockSpec((tm, KV_WIDTH), lambda i: (i, 0))
        kv_shape = jax.ShapeDtypeStruct((rows, KV_WIDTH), F32)
    else:
        layer_j, n_layers, prev_k, prev_v = layered
        if prev_k is None:
            assert layer_j == 0
            kv_spec = pl.BlockSpec((None, n_layers, tm, KV_WIDTH), lambda i: (i, 0, 0, 0))
        else:
            kv_spec = pl.BlockSpec((None, None, tm, KV_WIDTH), lambda i: (i, layer_j, 0, 0))
        kv_shape = jax.ShapeDtypeStruct((rows // tm, n_layers, tm, KV_WIDTH), F32)
        if prev_k is not None:
            aliases = {len(args): 1, len(args) + 1: 2}
            in_specs += [pl.BlockSpec(memory_space=pl.ANY), pl.BlockSpec(memory_space=pl.ANY)]
            args += [prev_k, prev_v]
    return pl.pallas_call(
        functools.partial(_attn_proj_kernel, use_rope=use_rope, self_attend=self_attend),
        grid=(rows // tm,),
        in_specs=in_specs,
        out_specs=[pl.BlockSpec((tm, Q_WIDTH), lambda i: (i, 0)), kv_spec, kv_spec],
        out_shape=[jax.ShapeDtypeStruct((rows, Q_WIDTH), BF16), kv_shape, kv_shape],
        input_output_aliases=aliases,
        compiler_params=_params(1),
        name="attn_proj",
    )(*args)


def _attend(q_head, kv_pieces, o_ref):
    for kv in range(KV_HEADS):
        ks, vs = kv_pieces(kv)
        heads = [kv * GROUP + g for g in range(GROUP)]
        scores = [[_dot_nt(q_head(hh), k) for k in ks] for hh in heads]
        probs, denoms = [], []
        for ss in scores:
            m = functools.reduce(jnp.maximum, [jnp.max(s, axis=-1, keepdims=True) for s in ss])
            ps = [jnp.exp(s - m) for s in ss]
            denoms.append(functools.reduce(jnp.add, [jnp.sum(p, axis=-1, keepdims=True) for p in ps]))
            probs.append([p.astype(BF16) for p in ps])
        outs = [functools.reduce(jnp.add, [_dot(p, v) for p, v in zip(ps, vs)]) for ps in probs]
        for hh, o, denom in zip(heads, outs, denoms):
            o_ref[:, hh * HEAD_DIM:(hh + 1) * HEAD_DIM] = (o / denom).astype(BF16)


def _attn_kernel(*refs, n_pieces):
    q_ref = refs[0]
    kv_refs = refs[1:1 + 2 * n_pieces]
    o_ref = refs[1 + 2 * n_pieces]
    q = q_ref[...]

    def kv_pieces(kv):
        sl = slice(kv * HEAD_DIM, (kv + 1) * HEAD_DIM)
        return ([kv_refs[2 * p][:, sl].astype(BF16) for p in range(n_pieces)],
                [kv_refs[2 * p + 1][:, sl].astype(BF16) for p in range(n_pieces)])

    _attend(lambda hh: q[:, hh * HEAD_DIM:(hh + 1) * HEAD_DIM], kv_pieces, o_ref)


def _latent_attention(q, k, v, cache_k4, cache_v4, layer_j, seq, tq):
    rows = q.shape[0]
    past = cache_k4.shape[2]
    nq = seq // tq
    cache_spec = pl.BlockSpec((None, None, past, KV_WIDTH), lambda b, i: (b, layer_j, 0, 0))
    return pl.pallas_call(
        functools.partial(_attn_kernel, n_pieces=2),
        grid=(rows // seq, nq),
        in_specs=[
            pl.BlockSpec((tq, Q_WIDTH), lambda b, i: (b * nq + i, 0)),
            cache_spec,
            cache_spec,
            pl.BlockSpec((seq, KV_WIDTH), lambda b, i: (b, 0)),
            pl.BlockSpec((seq, KV_WIDTH), lambda b, i: (b, 0)),
        ],
        out_specs=pl.BlockSpec((tq, Q_WIDTH), lambda b, i: (b * nq + i, 0)),
        out_shape=jax.ShapeDtypeStruct((rows, Q_WIDTH), BF16),
        compiler_params=_params(2),
        name="latent_attention",
    )(q, cache_k4, cache_v4, k, v)


def _dn_proj_kernel(x_ref, mods_ref, gain_ref, w_ref, wab_ref, alog_ref, dtb_ref, main_out, gb_out, gt_out):
    x = x_ref[...]
    h = (_rms(x, gain_ref[...]) * (1.0 + mods_ref[1:2, :]) + mods_ref[0:1, :]).astype(BF16)
    ab = _dot(h, wab_ref[...])
    main_out[...] = _dot(h, w_ref[...])
    lane = lax.broadcasted_iota(jnp.int32, ab.shape, 1)
    g = -jnp.exp(alog_ref[...]) * jax.nn.softplus(ab + dtb_ref[...])
    tm = ab.shape[0]
    r = lax.broadcasted_iota(jnp.int32, (tm, tm), 0)
    c = lax.broadcasted_iota(jnp.int32, (tm, tm), 1)
    same_chunk = (r // CHUNK) == (c // CHUNK)
    m_fwd = jnp.where(same_chunk, jnp.where(c <= r, 1.0, 0.0), 0.0).astype(BF16)
    m_bwd = jnp.where(same_chunk, jnp.where(c >= r, 1.0, 0.0), 0.0).astype(BF16)
    g1 = g.astype(BF16)
    rem = g - g1.astype(F32)
    g2 = rem.astype(BF16)
    g3 = (rem - g2.astype(F32)).astype(BF16)
    cum_f = _dot(m_fwd, g1) + (_dot(m_fwd, g2) + _dot(m_fwd, g3))
    cum_b = _dot(m_bwd, g1) + (_dot(m_bwd, g2) + _dot(m_bwd, g3))
    total = cum_f + cum_b - g
    out = jnp.where(lane < DN_HEADS, cum_f, jnp.where(lane < 2 * DN_HEADS, cum_b, jax.nn.sigmoid(ab)))
    table = jnp.where(lane < 4 * DN_HEADS, out, pltpu.roll(total, 4 * DN_HEADS, 1))
    gb_out[...] = table
    for ch in range(tm // CHUNK):
        gt_out[ch] = jnp.transpose(table[ch * CHUNK:(ch + 1) * CHUNK, :])


def _dn_proj(x, mods_l, cond_of_tile, gain, w_main_bf, w_ab_bf, alog_row, dtb_row, tm):
    rows = x.shape[0]
    return pl.pallas_call(
        _dn_proj_kernel,
        grid=(rows // tm,),
        in_specs=[
            pl.BlockSpec((tm, D_MODEL), lambda i: (i, 0)),
            _mods_spec(cond_of_tile),
            _const_spec((1, D_MODEL)),
            _const_spec((D_MODEL, DN_MAIN)),
            _const_spec((D_MODEL, LANES)),
            _const_spec((1, LANES)),
            _const_spec((1, LANES)),
        ],
        out_specs=[
            pl.BlockSpec((tm, DN_MAIN), lambda i: (i, 0)),
            pl.BlockSpec((tm, LANES), lambda i: (i, 0)),
            pl.BlockSpec((tm // CHUNK, LANES, CHUNK), lambda i: (i, 0, 0)),
        ],
        out_shape=[
            jax.ShapeDtypeStruct((rows, DN_MAIN), F32),
            jax.ShapeDtypeStruct((rows, LANES), F32),
            jax.ShapeDtypeStruct((rows // CHUNK, LANES, CHUNK), F32),
        ],
        compiler_params=_params(1),
        name="dn_proj",
    )(x, mods_l, gain.reshape(1, D_MODEL), w_main_bf, w_ab_bf, alog_row, dtb_row)


def _split_bf16(a):
    hi = a.astype(BF16)
    lo = (a - hi.astype(F32)).astype(BF16)
    return hi, lo


def _unit_tri_inverse_pair(low, eye, block_diag_mask):
    c = CHUNK

    def block_diag(x_bf):
        return jnp.where(block_diag_mask, jnp.concatenate([x_bf, x_bf], axis=0), jnp.zeros((), BF16))

    p = eye - low
    m = low
    steps = CHUNK.bit_length() - 2
    for it in range(steps + 1):
        need_square = it < steps
        need_pm = it > 0
        mh, ml = _split_bf16(m)
        lhs_a, lhs_b = [], []
        if need_square:
            lhs_a += [mh, ml]
            lhs_b += [mh]
        if need_pm:
            ph, pl_ = _split_bf16(p)
            lhs_a += [ph, pl_]
            lhs_b += [ph]
        a = _dot(jnp.concatenate(lhs_a, axis=0), block_diag(mh))
        b = _dot(jnp.concatenate(lhs_b, axis=0), block_diag(ml))
        off_a = off_b = 0
        if need_square:
            m = a[0:c] + a[c:2 * c] + b[0:c]
            off_a, off_b = 2 * c, c
        if need_pm:
            p = p + (a[off_a:off_a + c] + a[off_a + c:off_a + 2 * c] + b[off_b:off_b + c])
        yield
    return p


def _delta_kernel(*refs, seq, hg, has_s0, has_prev):
    (q_ref, k_ref, v_ref, z_ref, gb_ref, gt_ref, cq_ref, ck_ref, cv_ref, ng_ref) = refs[:10]
    pos = 10
    s0_ref = None
    if has_s0:
        s0_ref = refs[pos]
        pos += 1
    if has_prev:
        pos += 1
    og_ref = refs[pos]
    pos += 1
    st_ref = None
    if not has_s0:
        st_ref = refs[pos]
        pos += 1
    q_s, k_s, v_s, of_s, ob_s, pad_q, pad_k, pad_v, st_s = refs[pos:]

    head0 = pl.program_id(1) * hg
    n_chunks = seq // CHUNK
    halo = 8
    width = hg * DN_DK
    unrolled = n_chunks <= MAX_UNROLLED_CHUNKS

    for pad, x_ref in ((pad_q, q_ref), (pad_k, k_ref), (pad_v, v_ref)):
        pad[0:halo, :] = jnp.zeros((halo, width), F32)
        pad[halo + seq:2 * halo + seq, :] = jnp.zeros((halo, width), F32)
        pad[halo:halo + seq, :] = x_ref[...]

    def prep_chunk(c):
        r = c * CHUNK
        base = halo - CONV_K // 2 + r

        def conv_silu(pad, cw_ref):
            win = pad[r:r + CHUNK + 2 * halo, :]
            n_win = CHUNK + 2 * halo
            acc = None
            for j in range(CONV_K):
                shift = (CONV_K // 2 - j) % n_win
                tap = win if shift == 0 else pltpu.roll(win, shift, 0)
                term = tap[halo:halo + CHUNK, :] * cw_ref[j:j + 1, :]
                acc = term if acc is None else acc + term
            return _silu(acc)

        def l2n_heads(x, scale, dst):
            for h in range(hg):
                sl = slice(h * DN_DK, (h + 1) * DN_DK)
                xh = x[:, sl]
                dst[r:r + CHUNK, sl] = xh * (lax.rsqrt(jnp.sum(xh * xh, axis=-1, keepdims=True) + EPS) * scale)

        l2n_heads(conv_silu(pad_q, cq_ref), DN_DK ** -0.5, q_s)
        l2n_heads(conv_silu(pad_k, ck_ref), 1.0, k_s)
        v_s[r:r + CHUNK, :] = conv_silu(pad_v, cv_ref)

    order = []
    for c in range(n_chunks):
        for cc in (c, n_chunks - 1 - c):
            if cc not in order:
                order.append(cc)
    for c in order:
        prep_chunk(c)

    for d in range(2):
        for h in range(hg):
            if has_s0:
                st_s[d * hg + h] = s0_ref[d, h]
            else:
                st_s[d * hg + h] = jnp.zeros((DN_DK, DN_DK), F32)

    pair_w = 2 * CHUNK
    row_p = lax.broadcasted_iota(jnp.int32, (CHUNK, pair_w), 0)
    lane_p = lax.broadcasted_iota(jnp.int32, (CHUNK, pair_w), 1)
    col_p = lane_p & (CHUNK - 1)
    in_b = lane_p >= CHUNK
    eye = (row_p == col_p).astype(F32)
    bd_r = lax.broadcasted_iota(jnp.int32, (pair_w, pair_w), 0)
    bd_c = lax.broadcasted_iota(jnp.int32, (pair_w, pair_w), 1)
    block_diag_mask = (bd_r >= CHUNK) == (bd_c >= CHUNK)
    lane = lax.broadcasted_iota(jnp.int32, (CHUNK, LANES), 1)
    zeros_k = jnp.zeros((CHUNK, DN_DK), BF16)
    kq_cache = {}

    def pair_step(n, ha, direction):
        hb = ha + 1
        r0 = n * CHUNK if unrolled else pl.multiple_of(n * CHUNK, CHUNK)
        sls = [slice(h * DN_DK, (h + 1) * DN_DK) for h in (ha, hb)]
        qs = [q_s[pl.ds(r0, CHUNK), sl] for sl in sls]
        ks = [k_s[pl.ds(r0, CHUNK), sl] for sl in sls]
        vs = [v_s[pl.ds(r0, CHUNK), sl].astype(BF16) for sl in sls]
        ks_bf = [k.astype(BF16) for k in ks]
        gbc = gb_ref[pl.ds(r0, CHUNK), :]
        col_a = direction * DN_HEADS + head0 + ha

        def col_vec(c):
            return jnp.sum(jnp.where(lane == c, gbc, 0.0), axis=1, keepdims=True)

        def row_vec(r):
            both = gt_ref[n, pl.ds(r, 2), :]
            return jnp.concatenate([both[0:1], both[1:2]], axis=1)

        gcs = [col_vec(col_a), col_vec(col_a + 1)]
        gc_col = jnp.where(in_b, gcs[1], gcs[0])
        b_col = jnp.where(in_b, col_vec(col_a + 1 + 2 * DN_HEADS), col_vec(col_a + 2 * DN_HEADS))
        gc_row = row_vec(col_a)
        b_row = row_vec(col_a + 2 * DN_HEADS)
        tot = gt_ref[n, pl.ds(col_a + 4 * DN_HEADS, 2), :]
        g_tots = [tot[0:1, 0:1], tot[1:2, 0:1]]
        if direction == 0:
            incl = col_p <= row_p
            strict = col_p < row_p
        else:
            incl = col_p >= row_p
            strict = col_p > row_p
        decay = jnp.exp(jnp.where(incl, gc_col - gc_row, 0.0))
        kq = kq_cache.get((n, ha)) if unrolled else None
        if kq is None:
            kq = (_dot_nt(jnp.concatenate([ks_bf[0], qs[0].astype(BF16)], axis=0),
                          jnp.concatenate([ks_bf[0], zeros_k], axis=0))
                  + _dot_nt(jnp.concatenate([ks_bf[1], qs[1].astype(BF16)], axis=0),
                            jnp.concatenate([zeros_k, ks_bf[1]], axis=0)))
            if unrolled:
                kq_cache[(n, ha)] = kq
        yield
        low = jnp.where(strict, kq[0:CHUNK] * (decay * b_col), 0.0)
        qk = jnp.where(incl, kq[CHUNK:2 * CHUNK] * decay, 0.0).astype(BF16)
        tinv = yield from _unit_tri_inverse_pair(low, eye, block_diag_mask)
        t_u = (tinv * b_row).astype(BF16)
        t_w = (tinv * (b_row * jnp.exp(gc_row))).astype(BF16)
        pick = [lambda x: jnp.concatenate([x, zeros_k], axis=0), lambda x: jnp.concatenate([zeros_k, x], axis=0)]
        us = [_dot(t_u, pick[i](vs[i])) for i in range(2)]
        ws = [_dot(t_w, pick[i](ks_bf[i])) for i in range(2)]
        qgs = [(qs[i] * jnp.exp(gcs[i])).astype(BF16) for i in range(2)]
        kd = jnp.concatenate([ks[i] * jnp.exp(g_tots[i] - gcs[i]) for i in range(2)], axis=0)
        kd_t = jnp.transpose(kd).astype(BF16)
        yield
        return dict(us=us, ws=[w.astype(BF16) for w in ws], qgs=qgs, lhs_od=jnp.concatenate([qk, kd_t], axis=0),
                    decay_tot=[jnp.exp(g) for g in g_tots], pick=pick, r0=r0, sls=sls,
                    dst=of_s if direction == 0 else ob_s)

    def pair_scan(a, states):
        ws_qs = [_dot(jnp.concatenate([a["ws"][i], a["qgs"][i]], axis=0), states[i].astype(BF16))
                 for i in range(2)]
        yield
        v_new = [(a["us"][i] - ws_qs[i][0:CHUNK]).astype(BF16) for i in range(2)]
        ods = [_dot(a["lhs_od"], a["pick"][i](v_new[i])) for i in range(2)]
        yield
        new_states = [states[i] * a["decay_tot"][i] + ods[i][CHUNK:CHUNK + DN_DK] for i in range(2)]
        outs = [ws_qs[i][CHUNK:2 * CHUNK] + ods[i][0:CHUNK] for i in range(2)]
        return new_states, outs

    def round_robin(gens):
        results = [None] * len(gens)
        while any(r is None for r in results):
            for gi, gen in enumerate(gens):
                if results[gi] is None:
                    try:
                        next(gen)
                    except StopIteration as done:
                        results[gi] = done.value
        return results

    lanes_of = [(ha, d) for ha in range(0, hg, 2) for d in range(2)]

    def body(it, carry):
        def chunk_of(s, d):
            c = it * sub + s
            return c if d == 0 else n_chunks - 1 - c

        preps = round_robin([pair_step(chunk_of(s, d), ha, d) for s in range(sub) for ha, d in lanes_of])
        states = [[st_s[d * hg + ha], st_s[d * hg + ha + 1]] for ha, d in lanes_of]
        stores = []
        for s in range(sub):
            mine = preps[s * len(lanes_of):(s + 1) * len(lanes_of)]
            scanned = round_robin([pair_scan(a, st) for a, st in zip(mine, states)])
            states = [new for new, _ in scanned]
            stores += [(a, outs) for a, (_, outs) in zip(mine, scanned)]
        for (ha, d), st in zip(lanes_of, states):
            st_s[d * hg + ha] = st[0]
            st_s[d * hg + ha + 1] = st[1]
        for a, outs in stores:
            for i in range(2):
                a["dst"][pl.ds(a["r0"], CHUNK), a["sls"][i]] = outs[i]
        return carry

    if unrolled:
        sub = 1
        for n in range(n_chunks):
            body(n, 0)
    else:
        sub = CHUNKS_PER_STEP
        lax.fori_loop(0, n_chunks // sub, body, 0)
    if st_ref is not None:
        if len(st_ref.shape) == 5:
            for slot in range(1, st_ref.shape[0]):
                st_ref[slot] = jnp.zeros(st_ref.shape[1:], F32)
            st_ref = st_ref.at[0]
        for d in range(2):
            for h in range(hg):
                st_ref[d, h] = st_s[d * hg + h]

    for h in range(hg):
        sl = slice(h * DN_DK, (h + 1) * DN_DK)
        o = of_s[:, sl] + ob_s[:, sl]
        og_ref[:, sl] = (_rms(o, ng_ref[...]) * _silu(z_ref[:, sl])).astype(BF16)


def _delta_core(main, gb, gt, conv_w, norm_gain, s0, layer_j, seq, hg, n_layers=None, prev_states=None):
    rows = main.shape[0]
    n_seq = rows // seq
    n_chunks = seq // CHUNK
    has_s0 = s0 is not None
    n_groups = DN_HEADS // hg
    width = hg * DN_DK
    aliases = {}

    def col_spec(group):
        return pl.BlockSpec((seq, width), lambda b, g: (b, group * n_groups + g))

    def conv_spec(group):
        return pl.BlockSpec((CONV_K, width), lambda b, g: (0, group * n_groups + g))

    in_specs = [
        col_spec(0), col_spec(1), col_spec(2), col_spec(3),
        pl.BlockSpec((seq, LANES), lambda b, g: (b, 0)),
        pl.BlockSpec((n_chunks, LANES, CHUNK), lambda b, g: (b, 0, 0)),
        conv_spec(0), conv_spec(1), conv_spec(2),
        pl.BlockSpec((1, DN_DK), lambda b, g: (0, 0)),
    ]
    args = [main, main, main, main, gb, gt, conv_w, conv_w, conv_w, norm_gain.reshape(1, DN_DK)]
    out_specs = [pl.BlockSpec((seq, width), lambda b, g: (b, g))]
    out_shape = [jax.ShapeDtypeStruct((rows, DN_KEY), BF16)]
    if has_s0:
        in_specs.append(pl.BlockSpec((None, None, 2, hg, DN_DK, DN_DK), lambda b, g: (b, layer_j, 0, g, 0, 0)))
        args.append(s0)
    else:
        if prev_states is None:
            assert layer_j == 0
            out_specs.append(pl.BlockSpec((None, n_layers, 2, hg, DN_DK, DN_DK), lambda b, g: (b, 0, 0, g, 0, 0)))
        else:
            out_specs.append(pl.BlockSpec((None, None, 2, hg, DN_DK, DN_DK), lambda b, g: (b, layer_j, 0, g, 0, 0)))
        out_shape.append(jax.ShapeDtypeStruct((n_seq, n_layers, 2, DN_HEADS, DN_DK, DN_DK), F32))
        if prev_states is not None:
            aliases = {len(args): 1}
            in_specs.append(pl.BlockSpec(memory_space=pl.ANY))
            args.append(prev_states)
    scratch = [pltpu.VMEM((seq, width), F32) for _ in range(5)]
    scratch += [pltpu.VMEM((seq + 16, width), F32) for _ in range(3)]
    scratch.append(pltpu.VMEM((2 * hg, DN_DK, DN_DK), F32))
    return pl.pallas_call(
        functools.partial(_delta_kernel, seq=seq, hg=hg, has_s0=has_s0, has_prev=bool(aliases)),
        grid=(n_seq, n_groups),
        in_specs=in_specs,
        out_specs=out_specs,
        out_shape=out_shape,
        scratch_shapes=scratch,
        input_output_aliases=aliases,
        compiler_params=_params(2),
        name="delta_core",
    )(*args)


def _post_kernel(*refs, final, ff_chunk):
    x_ref, o_ref, mods_ref, g2_ref, wout_ref, w1_ref, w2_ref = refs[:7]
    fin_ref = refs[7] if final else None
    out_ref = refs[-1]
    x1 = x_ref[...] + mods_ref[2:3, :] * _dot(o_ref[...], wout_ref[...])
    h = (_rms(x1, g2_ref[...]) * (1.0 + mods_ref[4:5, :]) + mods_ref[3:4, :]).astype(BF16)
    acc = jnp.zeros(x1.shape, F32)
    for c in range(D_FF // ff_chunk):
        a = _dot(h, w1_ref[:, c * ff_chunk:(c + 1) * ff_chunk])
        a = jnp.square(jnp.maximum(a, 0.0)).astype(BF16)
        acc = acc + _dot(a, w2_ref[c * ff_chunk:(c + 1) * ff_chunk, :])
    x2 = x1 + mods_ref[5:6, :] * acc
    if final:
        x2 = _rms(x2, fin_ref[...])
    out_ref[...] = x2


def _post(x, o, mods_l, cond_of_tile, gain2, wout_bf, w1_bf, w2_bf, final_gain, tm):
    rows = x.shape[0]
    final = final_gain is not None
    in_specs = [
        pl.BlockSpec((tm, D_MODEL), lambda i: (i, 0)),
        pl.BlockSpec((tm, D_MODEL), lambda i: (i, 0)),
        _mods_spec(cond_of_tile),
        _const_spec((1, D_MODEL)),
        _resident_spec((D_MODEL, D_MODEL)),
        _resident_spec((D_MODEL, D_FF)),
        _resident_spec((D_FF, D_MODEL)),
    ]
    args = [x, o, mods_l, gain2.reshape(1, D_MODEL), wout_bf, w1_bf, w2_bf]
    if final:
        in_specs.append(_const_spec((1, D_MODEL)))
        args.append(final_gain.reshape(1, D_MODEL))
    return pl.pallas_call(
        functools.partial(_post_kernel, final=final, ff_chunk=1024),
        grid=(rows // tm,),
        in_specs=in_specs,
        out_specs=pl.BlockSpec((tm, D_MODEL), lambda i: (i, 0)),
        out_shape=jax.ShapeDtypeStruct((rows, D_MODEL), F32),
        compiler_params=_params(1),
        name="post_mlp",
    )(*args)


def _rope_tables(n_tokens):
    rows = n_tokens // GRID_W
    row = jnp.repeat(jnp.arange(rows), GRID_W).astype(F32)
    col = jnp.tile(jnp.arange(GRID_W), rows).astype(F32)
    inv = ROPE_THETA ** (-jnp.arange(ROPE_FREQS, dtype=F32) / ROPE_FREQS)
    ang_r = row[:, None] * inv
    ang_c = col[:, None] * inv
    cos = jnp.concatenate([jnp.cos(ang_r), jnp.cos(ang_r), jnp.cos(ang_c), jnp.cos(ang_c)], axis=1)
    sin = jnp.concatenate([-jnp.sin(ang_r), jnp.sin(ang_r), -jnp.sin(ang_c), jnp.sin(ang_c)], axis=1)
    return cos, sin


def kernel(x_prompt, x_sample, cache_k, cache_v, state_delta, c, c_ctx, w_mod, b_mod, norm1, norm2,
           attn_w_in, attn_q_gain, attn_k_gain, attn_w_out, dn_w_in, dn_conv, dn_a_log, dn_dt_bias,
           dn_norm, dn_w_out, mlp_w1, mlp_w2, final_norm):
    n_ctx, seq_ctx, _ = x_prompt.shape
    n_lat, seq_lat, _ = x_sample.shape
    past = cache_k.shape[2]
    tm = 256
    tm_post = 512

    cond = jnp.concatenate([c_ctx[None, :], c, jnp.zeros((N_COND - 1 - n_lat, D_MODEL), F32)], axis=0)
    mods = _modulation(cond, w_mod, b_mod).reshape(DEPTH, N_COND, N_MOD, D_MODEL)

    rope_cos, rope_sin = _rope_tables(seq_lat)
    lat_tiles = seq_lat // tm
    cache_k4 = cache_k.reshape(n_lat, cache_k.shape[1], past, KV_WIDTH)
    cache_v4 = cache_v.reshape(n_lat, cache_v.shape[1], past, KV_WIDTH)

    streams = {
        "ctx": dict(x=x_prompt.reshape(n_ctx * seq_ctx, D_MODEL), seq=seq_ctx,
                    cond=lambda rows_per_tile: (lambda i: 0), rope=None),
        "lat": dict(x=x_sample.reshape(n_lat * seq_lat, D_MODEL), seq=seq_lat,
                    cond=lambda rows_per_tile: (lambda i: 1 + i // (seq_lat // rows_per_tile)),
                    rope=lambda i: i % lat_tiles),
    }
    n_attn = (DEPTH + 1) // 2
    n_delta = DEPTH // 2
    new_k = new_v = new_states = None
    assert tm == seq_ctx

    for l in range(DEPTH):
        j = l // 2
        mods_l = mods[l]
        w1_bf = mlp_w1[l].astype(BF16)
        w2_bf = mlp_w2[l].astype(BF16)
        final_gain = final_norm if l == DEPTH - 1 else None
        if l % 2 == 0:
            w_in_bf = attn_w_in[j].astype(BF16)
            w_out_bf = attn_w_out[j].astype(BF16)
        else:
            w_main_bf = dn_w_in[j][:, :DN_MAIN].astype(BF16)
            w_ab_bf = jnp.pad(dn_w_in[j][:, DN_MAIN:], ((0, 0), (0, LANES - 4 * DN_HEADS))).astype(BF16)
            w_out_bf = dn_w_out[j].astype(BF16)
            alog_row = jnp.pad(dn_a_log[j].reshape(1, 2 * DN_HEADS), ((0, 0), (0, LANES - 2 * DN_HEADS)))
            dtb_row = jnp.pad(dn_dt_bias[j].reshape(1, 2 * DN_HEADS), ((0, 0), (0, LANES - 2 * DN_HEADS)))
        for name, st in streams.items():
            x = st["x"]
            seq = st["seq"]
            rows = x.shape[0]
            if l % 2 == 0:
                if name == "ctx":
                    o, new_k, new_v = _attn_proj(x, mods_l, st["cond"](tm), norm1[l], w_in_bf, attn_q_gain[j],
                                                 attn_k_gain[j], rope_cos, rope_sin, st["rope"], tm,
                                                 layered=(j, n_attn, new_k, new_v), self_attend=True)
                else:
                    q, k, v = _attn_proj(x, mods_l, st["cond"](tm), norm1[l], w_in_bf, attn_q_gain[j], attn_k_gain[j],
                                         rope_cos, rope_sin, st["rope"], tm)
                    o = _latent_attention(q, k, v, cache_k4, cache_v4, j, seq, tm)
            else:
                main, gb, gt = _dn_proj(x, mods_l, st["cond"](tm), norm1[l], w_main_bf, w_ab_bf, alog_row, dtb_row, tm)
                if name == "ctx":
                    o, new_states = _delta_core(main, gb, gt, dn_conv[j], dn_norm[j], None, j, seq, DN_HEADS,
                                                n_layers=n_delta, prev_states=new_states)
                else:
                    (o,) = _delta_core(main, gb, gt, dn_conv[j], dn_norm[j], state_delta, j, seq, DN_HEADS // 2)
            st["x"] = _post(x, o, mods_l, st["cond"](tm_post), norm2[l], w_out_bf, w1_bf, w2_bf, final_gain, tm_post)

    y_prompt = streams["ctx"]["x"].reshape(n_ctx, seq_ctx, D_MODEL)
    y_sample = streams["lat"]["x"].reshape(n_lat, seq_lat, D_MODEL)
    new_cache_k = new_k.reshape(n_ctx, n_attn, seq_ctx, KV_HEADS, HEAD_DIM)
    new_cache_v = new_v.reshape(n_ctx, n_attn, seq_ctx, KV_HEADS, HEAD_DIM)
    return (y_prompt, y_sample, new_cache_k, new_cache_v, new_states)
```
